```python
import math
import jax
import jax.numpy as jnp
from jax import lax
import numpy as np

D_MODEL = 1024
BATCH = 2
SEQ = 8192
DEPTH = 4
DEC_BATCH = 32
DEC_SEQ = 4
PAST_LEN = 8192
PAGE_SIZE = 128

N_MIXERS = 3
DIFF_HEADS = 8
DIFF_HEAD_DIM = 64
DIFF_V_DIM = 2 * DIFF_HEAD_DIM
SB_HEADS = 16
SB_HEAD_DIM = D_MODEL // SB_HEADS
MOBA_HEADS = 16
MOBA_HEAD_DIM = D_MODEL // MOBA_HEADS
MOBA_BLOCK = 256
MOBA_TOPK = 3
QK_WIDTH = D_MODEL
V_WIDTH = D_MODEL
KV_WIDTH = D_MODEL
IN_WIDTH = 2 * QK_WIDTH + 2 * V_WIDTH
Q_BLOCK = 128
MOBA_Q_BLOCK = 32
N_DIFF_LAYERS = (DEPTH + N_MIXERS - 1) // N_MIXERS
DEEPNORM_ALPHA = (2 * DEPTH) ** 0.25
DEEPNORM_BETA = (8 * DEPTH) ** -0.25
LN_EPS = 1e-5

kernel_name = 'hybrid_diff_stickbreak_moba_decode_step'


def alibi_slopes(n_heads):
    return jnp.asarray(2.0 ** (-8.0 * np.arange(1, n_heads + 1) / n_heads), dtype=jnp.float32)


def layer_norm(x, g, b):
    xf = x.astype(jnp.float32)
    mu = jnp.mean(xf, axis=-1, keepdims=True)
    var = jnp.mean(jnp.square(xf - mu), axis=-1, keepdims=True)
    return ((xf - mu) * lax.rsqrt(var + LN_EPS) * g.astype(jnp.float32) + b.astype(jnp.float32)).astype(x.dtype)


def diff_attend(q, k, v, q_pos, lam, lam_init, sub_gain):
    k_pos = jnp.arange(k.shape[1])
    s = jnp.einsum('bqhcd,bkhcd->bchqk', q, k).astype(jnp.float32) * DIFF_HEAD_DIM ** -0.5
    dist = (q_pos[:, None] - k_pos[None, :]).astype(jnp.float32)
    bias = -alibi_slopes(DIFF_HEADS)[:, None, None] * dist
    s = jnp.where(dist >= 0, s + bias, -jnp.inf)
    p = jax.nn.softmax(s, axis=-1)
    w = p[:, 0] - lam * p[:, 1]
    o = jnp.einsum('bhqk,bkhd->bqhd', w.astype(v.dtype), v).astype(jnp.float32)
    o = o * lax.rsqrt(jnp.mean(o * o, axis=-1, keepdims=True) + LN_EPS) * sub_gain.astype(jnp.float32)
    return (o * (1.0 - lam_init)).astype(q.dtype)


def sb_attend(q, k, v, q_pos):
    k_pos = jnp.arange(k.shape[1])
    z = jnp.einsum('bqhd,bkhd->bhqk', q, k).astype(jnp.float32) * SB_HEAD_DIM ** -0.5
    past = k_pos[None, :] < q_pos[:, None]
    log_keep = jnp.where(past, jax.nn.log_sigmoid(-z), 0.0)
    later = lax.cumsum(log_keep, axis=3, reverse=True) - log_keep
    a = jnp.where(past, jnp.exp(jax.nn.log_sigmoid(z) + later), 0.0)
    return jnp.einsum('bhqk,bkhd->bqhd', a.astype(v.dtype), v)


def moba_blocks(k, v):
    b, l, h, d = k.shape
    nb = -(-l // MOBA_BLOCK)
    pad = ((0, 0), (0, nb * MOBA_BLOCK - l), (0, 0), (0, 0))
    kp = jnp.pad(k, pad).reshape(b, nb, MOBA_BLOCK, h, d)
    vp = jnp.pad(v, pad).reshape(b, nb, MOBA_BLOCK, h, d)
    means = jnp.mean(kp.astype(jnp.float32), axis=2)
    return kp.transpose(0, 3, 1, 2, 4), vp.transpose(0, 3, 1, 2, 4), means


def moba_attend(q, q_pos, kb, vb, means):
    b, nq, h, d = q.shape
    nb = kb.shape[2]
    own = q_pos // MOBA_BLOCK
    gate = jnp.einsum('bqhd,bnhd->bqhn', q.astype(jnp.float32), means)
    fully_past = jnp.arange(nb)[None, :] < own[:, None]
    gate = jnp.where(fully_past[None, :, None, :], gate, -jnp.inf)
    n_sel = min(MOBA_TOPK, nb)
    _, top_idx = lax.top_k(gate, n_sel)
    own_idx = jnp.broadcast_to(own[None, :, None, None], (b, nq, h, 1)).astype(top_idx.dtype)
    idx = jnp.concatenate([top_idx, own_idx], axis=-1)
    slot_ok = jnp.concatenate([jnp.arange(n_sel)[None, :] < own[:, None],
                               jnp.ones((nq, 1), dtype=bool)], axis=1)
    bi = jnp.arange(b)[:, None, None, None]
    hi = jnp.arange(h)[None, None, :, None]
    kg = kb[bi, hi, idx]
    vg = vb[bi, hi, idx]
    s = jnp.einsum('bqhd,bqhsjd->bqhsj', q, kg).astype(jnp.float32) * MOBA_HEAD_DIM ** -0.5
    k_pos = idx[..., None] * MOBA_BLOCK + jnp.arange(MOBA_BLOCK)
    dist = (q_pos[None, :, None, None, None] - k_pos).astype(jnp.float32)
    slopes = alibi_slopes(MOBA_HEADS)[None, None, :, None, None]
    ok = slot_ok[None, :, None, :, None] & (dist >= 0)
    s = jnp.where(ok, s - slopes * dist, -jnp.inf)
    p = jax.nn.softmax(s.reshape(b, nq, h, -1), axis=-1).reshape(s.shape)
    return jnp.einsum('bqhsj,bqhsjd->bqhd', p.astype(vg.dtype), vg)


def sweep(fn, q, q_pos, block):
    b, n = q.shape[:2]
    nblk = n // block
    qb = jnp.swapaxes(q.reshape((b, nblk, block) + q.shape[2:]), 0, 1)
    pb = q_pos.reshape(nblk, block)
    out = lax.map(lambda args: fn(args[0], args[1]), (qb, pb))
    return jnp.swapaxes(out, 0, 1).reshape((b, n) + out.shape[3:])


def mixer_layer(i, x, q_pos, past_k, past_v, w_in_i, w_out_i, ln_g_i, ln_b_i,
                lam_q1, lam_k1, lam_q2, lam_k2, subln_gain, sweep_queries):
    b, n, _ = x.shape
    h = jnp.einsum('bsd,de->bse', x, w_in_i)
    q, k, v, g = jnp.split(h, [QK_WIDTH, 2 * QK_WIDTH, 2 * QK_WIDTH + V_WIDTH], axis=-1)
    if past_k is None:
        k_all, v_all = k, v
    else:
        k_all = jnp.concatenate([past_k.astype(k.dtype), k], axis=1)
        v_all = jnp.concatenate([past_v.astype(v.dtype), v], axis=1)
    kind = i % N_MIXERS
    if kind == 0:
        j = i // N_MIXERS
        lam_init = 0.8 - 0.6 * math.exp(-0.3 * i)
        lam = (jnp.exp(jnp.sum(lam_q1[j].astype(jnp.float32) * lam_k1[j].astype(jnp.float32)))
               - jnp.exp(jnp.sum(lam_q2[j].astype(jnp.float32) * lam_k2[j].astype(jnp.float32)))
               + lam_init)
        qh = q.reshape(b, n, DIFF_HEADS, 2, DIFF_HEAD_DIM)
        kh = k_all.reshape(b, -1, DIFF_HEADS, 2, DIFF_HEAD_DIM)
        vh = v_all.reshape(b, -1, DIFF_HEADS, DIFF_V_DIM)
        gain = subln_gain[j]
        fn = lambda qq, pp: diff_attend(qq, kh, vh, pp, lam, lam_init, gain)
        block = Q_BLOCK
    elif kind == 1:
        qh = q.reshape(b, n, SB_HEADS, SB_HEAD_DIM)
        kh = k_all.reshape(b, -1, SB_HEADS, SB_HEAD_DIM)
        vh = v_all.reshape(b, -1, SB_HEADS, SB_HEAD_DIM)
        fn = lambda qq, pp: sb_attend(qq, kh, vh, pp)
        block = Q_BLOCK
    else:
        qh = q.reshape(b, n, MOBA_HEADS, MOBA_HEAD_DIM)
        kb, vb, means = moba_blocks(k_all.reshape(b, -1, MOBA_HEADS, MOBA_HEAD_DIM),
                                    v_all.reshape(b, -1, MOBA_HEADS, MOBA_HEAD_DIM))
        fn = lambda qq, pp: moba_attend(qq, pp, kb, vb, means)
        block = MOBA_Q_BLOCK
    o = sweep(fn, qh, q_pos, block) if sweep_queries else fn(qh, q_pos)
    o = o.reshape(b, n, V_WIDTH) * jax.nn.silu(g)
    y = jnp.einsum('bse,ed->bsd', o, w_out_i)
    return layer_norm(DEEPNORM_ALPHA * x + y, ln_g_i, ln_b_i), k, v


def setup_inputs(seed: int = 0) -> dict:
    key = jax.random.key(seed)
    ks = jax.random.split(key, 16)
    n_pages = PAST_LEN // PAGE_SIZE
    n_used = DEC_BATCH * n_pages
    n_pool = n_used + max(1, n_used // 4)
    f32 = jnp.float32
    x_prompt = jax.random.normal(ks[0], (BATCH, SEQ, D_MODEL), f32)
    x_sample = jax.random.normal(ks[1], (DEC_BATCH, DEC_SEQ, D_MODEL), f32)
    cache_k = jax.random.normal(ks[2], (DEPTH, n_pool, PAGE_SIZE, KV_WIDTH), f32)
    cache_v = jax.random.normal(ks[3], (DEPTH, n_pool, PAGE_SIZE, KV_WIDTH), f32) * DEEPNORM_BETA
    page_table = jax.random.permutation(ks[4], n_pool)[:n_used].reshape(DEC_BATCH, n_pages).astype(jnp.int32)
    col_scale = jnp.concatenate([jnp.ones((2 * QK_WIDTH,), f32),
                                 jnp.full((V_WIDTH,), DEEPNORM_BETA, f32),
                                 jnp.ones((V_WIDTH,), f32)])
    w_in = jax.random.normal(ks[5], (DEPTH, D_MODEL, IN_WIDTH), f32) * (D_MODEL ** -0.5) * col_scale
    w_out = jax.random.normal(ks[6], (DEPTH, V_WIDTH, D_MODEL), f32) * (V_WIDTH ** -0.5) * DEEPNORM_BETA
    ln_gain = 1.0 + 0.02 * jax.random.normal(ks[7], (DEPTH, D_MODEL), f32)
    ln_bias = 0.02 * jax.random.normal(ks[8], (DEPTH, D_MODEL), f32)
    diff_lambda_q1 = 0.1 * jax.random.normal(ks[9], (N_DIFF_LAYERS, DIFF_HEAD_DIM), f32)
    diff_lambda_k1 = 0.1 * jax.random.normal(ks[10], (N_DIFF_LAYERS, DIFF_HEAD_DIM), f32)
    diff_lambda_q2 = 0.1 * jax.random.normal(ks[11], (N_DIFF_LAYERS, DIFF_HEAD_DIM), f32)
    diff_lambda_k2 = 0.1 * jax.random.normal(ks[12], (N_DIFF_LAYERS, DIFF_HEAD_DIM), f32)
    diff_subln_gain = 1.0 + 0.02 * jax.random.normal(ks[13], (N_DIFF_LAYERS, DIFF_V_DIM), f32)
    return {'x_prompt': x_prompt, 'x_sample': x_sample, 'cache_k': cache_k, 'cache_v': cache_v,
            'page_table': page_table, 'w_in': w_in, 'w_out': w_out, 'ln_gain': ln_gain,
            'ln_bias': ln_bias, 'diff_lambda_q1': diff_lambda_q1, 'diff_lambda_k1': diff_lambda_k1,
            'diff_lambda_q2': diff_lambda_q2, 'diff_lambda_k2': diff_lambda_k2,
            'diff_subln_gain': diff_subln_gain}


def reference(x_prompt, x_sample, cache_k, cache_v, page_table, w_in, w_out, ln_gain, ln_bias,
              diff_lambda_q1, diff_lambda_k1, diff_lambda_q2, diff_lambda_k2, diff_subln_gain):
    dec_batch, n_pages = page_table.shape
    past_len = n_pages * PAGE_SIZE
    pos_prompt = jnp.arange(x_prompt.shape[1], dtype=jnp.int32)
    pos_sample = past_len + jnp.arange(x_sample.shape[1], dtype=jnp.int32)
    xp, xs = x_prompt, x_sample
    kp_rows, vp_rows, ks_rows, vs_rows = [], [], [], []
    for i in range(DEPTH):
        past_k = cache_k[i][page_table].reshape(dec_batch, past_len, KV_WIDTH)
        past_v = cache_v[i][page_table].reshape(dec_batch, past_len, KV_WIDTH)
        xp, kp, vp = mixer_layer(i, xp, pos_prompt, None, None, w_in[i], w_out[i], ln_gain[i], ln_bias[i],
                                 diff_lambda_q1, diff_lambda_k1, diff_lambda_q2, diff_lambda_k2,
                                 diff_subln_gain, True)
        xs, k_new, v_new = mixer_layer(i, xs, pos_sample, past_k, past_v, w_in[i], w_out[i], ln_gain[i],
                                       ln_bias[i], diff_lambda_q1, diff_lambda_k1, diff_lambda_q2,
                                       diff_lambda_k2, diff_subln_gain, False)
        kp_rows.append(kp)
        vp_rows.append(vp)
        ks_rows.append(k_new)
        vs_rows.append(v_new)
    new_k_prompt = jnp.stack(kp_rows)
    new_v_prompt = jnp.stack(vp_rows)
    new_k_sample = jnp.stack(ks_rows)
    new_v_sample = jnp.stack(vs_rows)
    return (xp, xs, new_k_prompt, new_v_prompt, new_k_sample, new_v_sample)
```

```python
import functools
import math

import numpy as np
import jax
import jax.numpy as jnp
from jax import lax
from jax.experimental import pallas as pl
from jax.experimental.pallas import tpu as pltpu

F32 = jnp.float32
BF16 = jnp.bfloat16

N_MIXERS = 3
DIFF_HEADS = 8
HEAD_DIM = 64
SB_HEADS = 16
MOBA_HEADS = 16
MOBA_BLOCK = 256
MOBA_TOPK = 3
LN_EPS = 1e-5
QK_SCALE = HEAD_DIM ** -0.5

LANES = 128
VMEM_LIMIT = 56 * 1024 * 1024

SB_LOG_ZERO = -104.5

KIND_DIFF, KIND_SB, KIND_MOBA = 0, 1, 2
NEG_INF = float("-inf")


def _alibi_slopes(n_heads):
    return np.asarray(2.0 ** (-8.0 * np.arange(1, n_heads + 1) / n_heads), dtype=np.float32)


def _dot_nt(a, b):
    return lax.dot_general(a, b, (((1,), (1,)), ((), ())), preferred_element_type=F32)


def _split_maps(q):
    lane = lax.broadcasted_iota(jnp.int32, q.shape, 1)
    return [jnp.where(lane < HEAD_DIM, q, 0.0), jnp.where(lane >= HEAD_DIM, q, 0.0)]


def _diff_lambda(lq1, lk1, lq2, lk2, lam_init):
    a = jnp.exp(jnp.sum(lq1[...] * lk1[...], axis=-1, keepdims=True))
    b = jnp.exp(jnp.sum(lq2[...] * lk2[...], axis=-1, keepdims=True))
    return a - b + lam_init


def _top_select(gate, valid):
    lane = lax.broadcasted_iota(jnp.int32, gate.shape, 1).astype(F32)
    g = jnp.where(valid, gate, NEG_INF)
    sel = jnp.zeros(gate.shape, F32)
    for _ in range(MOBA_TOPK):
        m = jnp.max(g, axis=-1, keepdims=True)
        idx = jnp.min(jnp.where(g == m, lane, 1e9), axis=-1, keepdims=True)
        idx = jnp.where(m > NEG_INF, idx, 1e9)
        pick = lane == idx
        sel = jnp.where(pick, 1.0, sel)
        g = jnp.where(pick, NEG_INF, g)
    return sel


def _softplus(z):
    return jnp.maximum(z, 0.0) + jnp.log1p(jnp.exp(-jnp.abs(z)))


def _suffix_matrix(tk):
    j = lax.broadcasted_iota(jnp.int32, (2 * tk, tk), 0)
    s = lax.broadcasted_iota(jnp.int32, (2 * tk, tk), 1)
    return jnp.where((j & (tk - 1)) > s, 1.0, 0.0).astype(BF16)


def _suffix_sum(lk, t2):
    hi = lk.astype(BF16)
    lo = (lk - hi.astype(F32)).astype(BF16)
    return jnp.dot(jnp.concatenate([hi, lo], axis=1), t2, preferred_element_type=F32)


def _in_proj_kernel(x_ref, w_ref, q_ref, k_ref, v_ref, g_ref, kb_ref, vb_ref):
    xb = x_ref[...].astype(BF16)
    d = q_ref.shape[-1]
    for j, o_ref in enumerate((q_ref, k_ref, v_ref, g_ref)):
        r = jnp.dot(xb, w_ref[:, j * d:(j + 1) * d], preferred_element_type=F32)
        o_ref[...] = r
        if j == 1:
            kb_ref[...] = r.astype(BF16)
        if j == 2:
            vb_ref[...] = r.astype(BF16)


def _in_proj(x2d, w_bf16, tm):
    n, d = x2d.shape
    row = pl.BlockSpec((tm, d), lambda i: (i, 0))
    return pl.pallas_call(
        _in_proj_kernel,
        grid=(n // tm,),
        in_specs=[row, pl.BlockSpec((d, 4 * d), lambda i: (0, 0))],
        out_specs=[row] * 6,
        out_shape=[jax.ShapeDtypeStruct((n, d), F32)] * 4 + [jax.ShapeDtypeStruct((n, d), BF16)] * 2,
        compiler_params=pltpu.CompilerParams(vmem_limit_bytes=VMEM_LIMIT),
        name="in_proj",
    )(x2d, w_bf16)


def _out_proj_kernel(o_ref, g_ref, x_ref, w_ref, gain_ref, bias_ref, y_ref, *, alpha):
    gt = g_ref[...]
    og = o_ref[...] * (gt / (1.0 + jnp.exp(-gt)))
    y = jnp.dot(og.astype(BF16), w_ref[...], preferred_element_type=F32)
    z = alpha * x_ref[...] + y
    mu = jnp.mean(z, axis=-1, keepdims=True)
    zc = z - mu
    var = jnp.mean(zc * zc, axis=-1, keepdims=True)
    y_ref[...] = zc * lax.rsqrt(var + LN_EPS) * gain_ref[...] + bias_ref[...]


def _out_proj(o2d, g2d, x2d, w_bf16, gain, bias, alpha, tm):
    n, d = x2d.shape
    row = pl.BlockSpec((tm, d), lambda i: (i, 0))
    vec = pl.BlockSpec((1, d), lambda i: (0, 0))
    return pl.pallas_call(
        functools.partial(_out_proj_kernel, alpha=alpha),
        grid=(n // tm,),
        in_specs=[row, row, row, pl.BlockSpec((d, d), lambda i: (0, 0)), vec, vec],
        out_specs=row,
        out_shape=jax.ShapeDtypeStruct((n, d), F32),
        compiler_params=pltpu.CompilerParams(vmem_limit_bytes=VMEM_LIMIT),
        name="out_proj_ln",
    )(o2d, g2d, x2d, w_bf16, gain.reshape(1, d), bias.reshape(1, d))


def _online_softmax_update(c, s, vt, m_sc, l_sc, acc_sc):
    m_prev = m_sc[c]
    m_new = jnp.maximum(m_prev, jnp.max(s, axis=-1, keepdims=True))
    p = jnp.exp(s - m_new)
    alpha = jnp.exp(m_prev - m_new)
    l_sc[c] = alpha * l_sc[c] + jnp.sum(p, axis=-1, keepdims=True)
    acc_sc[c] = alpha * acc_sc[c] + jnp.dot(p.astype(BF16), vt, preferred_element_type=F32)
    m_sc[c] = m_new


def _diff_prompt_kernel(slope_ref, q_ref, k_ref, v_ref, lq1, lk1, lq2, lk2, gain_ref, o_ref,
                        m_sc, l_sc, acc_sc, *, t, lam_init):
    g = pl.program_id(1)
    i = pl.program_id(2)
    slope = slope_ref[g]
    qm = [x.astype(BF16) for x in _split_maps(q_ref[...] * QK_SCALE)]
    m_sc[...] = jnp.full(m_sc.shape, NEG_INF, F32)
    l_sc[...] = jnp.zeros(l_sc.shape, F32)
    acc_sc[...] = jnp.zeros(acc_sc.shape, F32)
    q0 = i * t
    rowpos = q0 + lax.broadcasted_iota(jnp.int32, (t, 1), 0)

    def tile(j, masked):
        k0 = pl.multiple_of(j * t, t)
        kt = k_ref[pl.ds(k0, t), :]
        vt = v_ref[pl.ds(k0, t), :]
        colpos = k0 + lax.broadcasted_iota(jnp.int32, (1, t), 1)
        colbias = slope * (colpos - q0).astype(F32)
        for c in range(2):
            s = _dot_nt(qm[c], kt) + colbias
            if masked:
                s = jnp.where(colpos <= rowpos, s, NEG_INF)
            _online_softmax_update(c, s, vt, m_sc, l_sc, acc_sc)

    def body(j, carry):
        tile(j, False)
        return carry

    lax.fori_loop(0, i, body, 0)
    tile(i, True)

    lam = _diff_lambda(lq1, lk1, lq2, lk2, lam_init)
    o = acc_sc[0] / l_sc[0] - lam * (acc_sc[1] / l_sc[1])
    ms = jnp.mean(o * o, axis=-1, keepdims=True)
    o_ref[...] = o * lax.rsqrt(ms + LN_EPS) * gain_ref[...] * (1.0 - lam_init)


def _sb_prompt_kernel(q_ref, k_ref, v_ref, o_ref, c_sc, acc_sc, *, tq, tk):
    i = pl.program_id(2)
    qm = [x.astype(BF16) for x in _split_maps(q_ref[...] * QK_SCALE)]
    c_sc[...] = jnp.zeros(c_sc.shape, F32)
    acc_sc[...] = jnp.zeros(acc_sc.shape, F32)
    rowpos = i * tq + lax.broadcasted_iota(jnp.int32, (tq, 1), 0)
    t2 = _suffix_matrix(tk)

    def chunk(j):
        k0 = pl.multiple_of(j * tk, tk)
        kt = k_ref[pl.ds(k0, tk), :]
        vt = v_ref[pl.ds(k0, tk), :]
        colpos = k0 + lax.broadcasted_iota(jnp.int32, (1, tk), 1)
        past = colpos < rowpos
        for h in range(2):
            z = _dot_nt(qm[h], kt)
            lk = jnp.where(past, -_softplus(z), 0.0)
            c_prev = c_sc[h]
            later = c_prev + _suffix_sum(lk, t2)
            a = jnp.where(past, jnp.exp(z + lk + later), 0.0)
            acc_sc[h] = acc_sc[h] + jnp.dot(a.astype(BF16), vt, preferred_element_type=F32)
            c_sc[h] = c_prev + jnp.sum(lk, axis=-1, keepdims=True)

    def cond(carry):
        j, live = carry
        return jnp.logical_and(j >= 0, live > 0)

    def body(carry):
        j, _ = carry
        chunk(j)
        live = jnp.max(jnp.maximum(c_sc[0], c_sc[1])) > SB_LOG_ZERO
        return j - 1, live.astype(jnp.int32)

    r = tq // tk
    lax.while_loop(cond, body, (i * r + r - 1, jnp.int32(1)))
    lane = lax.broadcasted_iota(jnp.int32, (tq, LANES), 1)
    o_ref[...] = jnp.where(lane < HEAD_DIM, acc_sc[0], acc_sc[1])


def _moba_prompt_kernel(slope_ref, q_ref, k_ref, v_ref, mean_ref, o_ref,
                        m_sc, l_sc, acc_sc, sel_sc, *, t):
    g = pl.program_id(1)
    i = pl.program_id(2)
    qf = _split_maps(q_ref[...])
    qm = [(x * QK_SCALE).astype(BF16) for x in qf]
    means = mean_ref[...]
    nb = means.shape[0]
    blk = lax.broadcasted_iota(jnp.int32, (t, nb), 1)
    for h in range(2):
        gate = lax.dot_general(qf[h], means, (((1,), (1,)), ((), ())),
                               precision=lax.Precision.HIGHEST, preferred_element_type=F32)
        sel_sc[h] = _top_select(gate, blk < i)
    m_sc[...] = jnp.full(m_sc.shape, NEG_INF, F32)
    l_sc[...] = jnp.zeros(l_sc.shape, F32)
    acc_sc[...] = jnp.zeros(acc_sc.shape, F32)
    q0 = i * t
    rowpos = q0 + lax.broadcasted_iota(jnp.int32, (t, 1), 0)

    def tile(j, own):
        k0 = pl.multiple_of(j * t, t)
        kt = k_ref[pl.ds(k0, t), :]
        vt = v_ref[pl.ds(k0, t), :]
        colpos = k0 + lax.broadcasted_iota(jnp.int32, (1, t), 1)
        rel = (colpos - q0).astype(F32)
        for h in range(2):
            s = _dot_nt(qm[h], kt) + slope_ref[2 * g + h] * rel
            if own:
                s = jnp.where(colpos <= rowpos, s, NEG_INF)
            else:
                picked = jnp.max(jnp.where(blk == j, sel_sc[h], 0.0), axis=-1, keepdims=True)
                s = jnp.where(picked > 0.0, s, NEG_INF)
            _online_softmax_update(h, s, vt, m_sc, l_sc, acc_sc)

    def body(j, carry):
        tile(j, False)
        return carry

    tile(i, True)
    lax.fori_loop(0, i, body, 0)
    lane = lax.broadcasted_iota(jnp.int32, (t, LANES), 1)
    o_ref[...] = jnp.where(lane < HEAD_DIM, acc_sc[0] / l_sc[0], acc_sc[1] / l_sc[1])


def _block_mean_kernel(k_ref, o_ref, *, per_step):
    k = k_ref[...]
    o_ref[...] = jnp.mean(k.reshape(per_step, MOBA_BLOCK, k.shape[-1]), axis=1)


def _block_means(k3d):
    b, s, d = k3d.shape
    nb = s // MOBA_BLOCK
    per_step = 8 if nb % 8 == 0 else nb
    return pl.pallas_call(
        functools.partial(_block_mean_kernel, per_step=per_step),
        grid=(b, nb // per_step),
        in_specs=[pl.BlockSpec((None, per_step * MOBA_BLOCK, d), lambda bi, n: (bi, n, 0))],
        out_specs=pl.BlockSpec((None, per_step, d), lambda bi, n: (bi, n, 0)),
        out_shape=jax.ShapeDtypeStruct((b, nb, d), F32),
        compiler_params=pltpu.CompilerParams(vmem_limit_bytes=VMEM_LIMIT),
        name="moba_block_means",
    )(k3d)


def _prompt_mixer(kind, layer, q, k, kb, vb, lam_vecs, sub_gain):
    b, s, d = q.shape
    groups = d // LANES
    smem = pl.BlockSpec(memory_space=pltpu.SMEM)
    kv_spec = pl.BlockSpec((None, s, LANES), lambda bi, g, i: (bi, 0, g))

    def qo_spec(t):
        return pl.BlockSpec((None, t, LANES), lambda bi, g, i: (bi, i, g))

    params = pltpu.CompilerParams(vmem_limit_bytes=VMEM_LIMIT)
    out_shape = jax.ShapeDtypeStruct((b, s, d), F32)
    if kind == KIND_DIFF:
        t = min(512, s)
        lam_init = 0.8 - 0.6 * math.exp(-0.3 * layer)
        vec = pl.BlockSpec((1, HEAD_DIM), lambda bi, g, i: (0, 0))
        return pl.pallas_call(
            functools.partial(_diff_prompt_kernel, t=t, lam_init=lam_init),
            grid=(b, groups, s // t),
            in_specs=[smem, qo_spec(t), kv_spec, kv_spec, vec, vec, vec, vec,
                      pl.BlockSpec((1, LANES), lambda bi, g, i: (0, 0))],
            out_specs=qo_spec(t),
            out_shape=out_shape,
            scratch_shapes=[pltpu.VMEM((2, t, 1), F32), pltpu.VMEM((2, t, 1), F32),
                            pltpu.VMEM((2, t, LANES), F32)],
            compiler_params=params,
            name="diff_prompt",
        )(jnp.asarray(_alibi_slopes(DIFF_HEADS)), q, kb, vb, *lam_vecs, sub_gain)
    if kind == KIND_SB:
        tq, tk = min(256, s), 128
        return pl.pallas_call(
            functools.partial(_sb_prompt_kernel, tq=tq, tk=tk),
            grid=(b, groups, s // tq),
            in_specs=[qo_spec(tq), kv_spec, kv_spec],
            out_specs=qo_spec(tq),
            out_shape=out_shape,
            scratch_shapes=[pltpu.VMEM((2, tq, 1), F32), pltpu.VMEM((2, tq, LANES), F32)],
            compiler_params=params,
            name="sb_prompt",
        )(q, kb, vb)
    t = MOBA_BLOCK
    nb = s // t
    means = _block_means(k)
    return pl.pallas_call(
        functools.partial(_moba_prompt_kernel, t=t),
        grid=(b, groups, nb),
        in_specs=[smem, qo_spec(t), kv_spec, kv_spec,
                  pl.BlockSpec((None, nb, LANES), lambda bi, g, i: (bi, 0, g))],
        out_specs=qo_spec(t),
        out_shape=out_shape,
        scratch_shapes=[pltpu.VMEM((2, t, 1), F32), pltpu.VMEM((2, t, 1), F32),
                        pltpu.VMEM((2, t, LANES), F32), pltpu.VMEM((2, t, nb), F32)],
        compiler_params=params,
        name="moba_prompt",
    )(jnp.asarray(_alibi_slopes(MOBA_HEADS)), q, kb, vb, means)


REPL = 16


def _row_segment(r, kind):
    rep = r % REPL
    if kind == KIND_DIFF:
        return 2 * (rep % DIFF_HEADS) + rep // DIFF_HEADS
    return rep


def _sample_kernel(pt_ref, qrep_ref, knew_ref, vnew_ref, slope_ref, lq1, lk1, lq2, lk2, gain_ref,
                   *rest, kind, pages, nkc, past, n_new, lam_init):
    k_refs = rest[:pages]
    v_refs = rest[pages:2 * pages]
    o_ref = rest[2 * pages]
    wq_sc, wqf_sc, s_sc, p_sc, acc_sc, gate_sc = rest[2 * pages + 1:]
    del pt_ref
    step = pl.program_id(1)
    page = k_refs[0].shape[0]
    rows, d = wqf_sc.shape
    width = s_sc.shape[1]
    out_rows = rows // 2 if kind == KIND_DIFF else rows
    row_id = lax.broadcasted_iota(jnp.int32, (rows, 1), 0)
    qpos = past + row_id // REPL

    @pl.when(step == 0)
    def _():
        col = lax.broadcasted_iota(jnp.int32, (1, d), 1)
        wq = jnp.where(col // HEAD_DIM == _row_segment(row_id, kind), qrep_ref[...], 0.0)
        wqf_sc[...] = wq
        wq_sc[...] = (wq * QK_SCALE).astype(BF16)
        gate_sc[...] = jnp.zeros(gate_sc.shape, F32)

    @pl.when(step < nkc)
    def _():
        for pp in range(0, pages, 2):
            k0 = k_refs[pp][...]
            k1 = k_refs[pp + 1][...]
            kk = jnp.concatenate([k0.astype(BF16), k1.astype(BF16)], axis=0)
            col0 = pl.multiple_of((step * pages + pp) * page, 2 * page)
            s_sc[:, pl.ds(col0, 2 * page)] = _dot_nt(wq_sc[...], kk)
            if kind == KIND_MOBA:
                mean = (jnp.sum(k0, axis=0, keepdims=True) + jnp.sum(k1, axis=0, keepdims=True)) * (1.0 / MOBA_BLOCK)
                gcol = jnp.sum(wqf_sc[...] * mean, axis=-1, keepdims=True)
                lane = lax.broadcasted_iota(jnp.int32, gate_sc.shape, 1)
                gate_sc[...] = jnp.where(lane == (step * pages + pp) // 2, gcol, gate_sc[...])

    @pl.when(step == nkc)
    def _():
        s_sc[:, past:past + page] = _dot_nt(wq_sc[...], knew_ref[...])
        col = lax.broadcasted_iota(jnp.int32, (1, width), 1)
        if kind == KIND_SB:
            t2 = _suffix_matrix(page)
            n_chunks = width // page

            def body(it, c_prev):
                c0 = pl.multiple_of((n_chunks - 1 - it) * page, page)
                z = s_sc[:, pl.ds(c0, page)]
                colpos = c0 + lax.broadcasted_iota(jnp.int32, (1, page), 1)
                prior = colpos < qpos
                lk = jnp.where(prior, -_softplus(z), 0.0)
                later = c_prev + _suffix_sum(lk, t2)
                p_sc[:, pl.ds(c0, page)] = jnp.where(prior, jnp.exp(z + lk + later), 0.0)
                return c_prev + jnp.sum(lk, axis=-1, keepdims=True)

            lax.fori_loop(0, n_chunks, body, jnp.zeros((rows, 1), F32))
        else:
            dist = (qpos - col).astype(F32)
            ok = dist >= 0.0
            if kind == KIND_MOBA:
                own = past // MOBA_BLOCK
                lane = lax.broadcasted_iota(jnp.int32, gate_sc.shape, 1)
                sel = _top_select(gate_sc[...], lane < own)
                sel = jnp.where(lane == own, 1.0, sel)
                blk_row = lax.broadcasted_iota(jnp.int32, (gate_sc.shape[1], 1), 0)
                expand = jnp.where(col // MOBA_BLOCK == blk_row, 1.0, 0.0).astype(BF16)
                picked = jnp.dot(sel.astype(BF16), expand, preferred_element_type=F32)
                ok = jnp.logical_and(ok, picked > 0.5)
            sb = jnp.where(ok, s_sc[...] - slope_ref[...] * dist, NEG_INF)
            e = jnp.exp(sb - jnp.max(sb, axis=-1, keepdims=True))
            p = e / jnp.sum(e, axis=-1, keepdims=True)
            if kind == KIND_DIFF:
                lam = _diff_lambda(lq1, lk1, lq2, lk2, lam_init)
                half = REPL // 2
                for qi in range(n_new):
                    p_sc[qi * half:(qi + 1) * half, :] = (
                        p[qi * REPL:qi * REPL + half] - lam * p[qi * REPL + half:(qi + 1) * REPL])
            else:
                p_sc[...] = p
        acc_sc[0:out_rows, :] = jnp.dot(p_sc[0:out_rows, past:past + page].astype(BF16), vnew_ref[...],
                                        preferred_element_type=F32)

    @pl.when(step >= nkc)
    def _():
        for pp in range(0, pages, 2):
            vv = jnp.concatenate([v_refs[pp][...].astype(BF16), v_refs[pp + 1][...].astype(BF16)], axis=0)
            col0 = pl.multiple_of(((step - nkc) * pages + pp) * page, 2 * page)
            w = p_sc[0:out_rows, pl.ds(col0, 2 * page)].astype(BF16)
            acc_sc[0:out_rows, :] = acc_sc[0:out_rows, :] + jnp.dot(w, vv, preferred_element_type=F32)

    @pl.when(step == 2 * nkc - 1)
    def _():
        per_q = out_rows // n_new
        seg_w = LANES if kind == KIND_DIFF else HEAD_DIM
        r = lax.broadcasted_iota(jnp.int32, (per_q, 1), 0)
        col = lax.broadcasted_iota(jnp.int32, (1, d), 1)
        own_cols = col // seg_w == r
        for qi in range(n_new):
            a = jnp.where(own_cols, acc_sc[qi * per_q:(qi + 1) * per_q, :], 0.0)
            if kind == KIND_DIFF:
                ms = jnp.sum(a * a, axis=-1, keepdims=True) * (1.0 / LANES)
                a = a * lax.rsqrt(ms + LN_EPS) * gain_ref[...] * (1.0 - lam_init)
            o_ref[qi:qi + 1, :] = jnp.sum(a, axis=0, keepdims=True)


def _sample_mixer(kind, layer, page_table, q, k_new, v_new, cache_k, cache_v, lam_vecs, sub_gain):
    db, n_new, d = q.shape
    n_pages = page_table.shape[1]
    page = cache_k.shape[2]
    past = n_pages * page
    pages = 8 if n_pages % 8 == 0 else 2
    nkc = n_pages // pages
    rows = n_new * REPL
    width = past + page
    assert MOBA_BLOCK == 2 * page and past % MOBA_BLOCK == 0 and past // MOBA_BLOCK < LANES
    assert n_pages % pages == 0 and n_new <= page and rows % 16 == 0
    lam_init = 0.8 - 0.6 * math.exp(-0.3 * layer)

    qrep = jnp.repeat(q, REPL, axis=1)
    pad = ((0, 0), (0, page - n_new), (0, 0))
    knew = jnp.pad(k_new, pad).astype(BF16)
    vnew = jnp.pad(v_new, pad).astype(BF16)
    rep = np.arange(rows) % REPL
    if kind == KIND_DIFF:
        slopes = _alibi_slopes(DIFF_HEADS)[rep % DIFF_HEADS]
    elif kind == KIND_MOBA:
        slopes = _alibi_slopes(MOBA_HEADS)[rep]
    else:
        slopes = np.zeros((rows,), np.float32)
    slope_rows = jnp.asarray(slopes.reshape(rows, 1))
    gain_full = jnp.tile(sub_gain, (1, d // LANES))

    def per_sample(shape):
        return pl.BlockSpec((None,) + shape, lambda b, s, pt: (b, 0, 0))

    def const(shape):
        return pl.BlockSpec(shape, lambda b, s, pt: (0, 0))

    def k_spec(p):
        return pl.BlockSpec(
            (None, None, page, d),
            lambda b, s, pt: (layer, pt[b * n_pages + jnp.minimum(s, nkc - 1) * pages + p], 0, 0))

    def v_spec(p):
        return pl.BlockSpec(
            (None, None, page, d),
            lambda b, s, pt: (layer, pt[b * n_pages + jnp.maximum(s - nkc, 0) * pages + p], 0, 0))

    vec = const((1, HEAD_DIM))
    grid_spec = pltpu.PrefetchScalarGridSpec(
        num_scalar_prefetch=1,
        grid=(db, 2 * nkc),
        in_specs=[per_sample((rows, d)), per_sample((page, d)), per_sample((page, d)),
                  const((rows, 1)), vec, vec, vec, vec, const((1, d))]
                 + [k_spec(p) for p in range(pages)] + [v_spec(p) for p in range(pages)],
        out_specs=per_sample((n_new, d)),
        scratch_shapes=[pltpu.VMEM((rows, d), BF16), pltpu.VMEM((rows, d), F32),
                        pltpu.VMEM((rows, width), F32), pltpu.VMEM((rows, width), F32),
                        pltpu.VMEM((rows, d), F32), pltpu.VMEM((rows, LANES), F32)],
    )
    return pl.pallas_call(
        functools.partial(_sample_kernel, kind=kind, pages=pages, nkc=nkc, past=past,
                          n_new=n_new, lam_init=lam_init),
        grid_spec=grid_spec,
        out_shape=jax.ShapeDtypeStruct((db, n_new, d), F32),
        compiler_params=pltpu.CompilerParams(vmem_limit_bytes=VMEM_LIMIT),
        name=("diff_sample", "sb_sample", "moba_sample")[kind],
    )(page_table.reshape(-1), qrep, knew, vnew, slope_rows, *lam_vecs, gain_full,
      *([cache_k] * pages), *([cache_v] * pages))


def kernel(x_prompt, x_sample, cache_k, cache_v, page_table, w_in, w_out, ln_gain, ln_bias,
           diff_lambda_q1, diff_lambda_k1, diff_lambda_q2, diff_lambda_k2, diff_subln_gain):
    depth = w_in.shape[0]
    b, s, d = x_prompt.shape
    db, ds, _ = x_sample.shape
    alpha = (2 * depth) ** 0.25
    tm = min(512, b * s)
    xp = x_prompt.reshape(b * s, d)
    xs = x_sample.reshape(db * ds, d)
    kp_rows, vp_rows, ks_rows, vs_rows = [], [], [], []
    for i in range(depth):
        kind = i % N_MIXERS
        j = i // N_MIXERS
        lam_vecs = [v[j].reshape(1, HEAD_DIM) for v in
                    (diff_lambda_q1, diff_lambda_k1, diff_lambda_q2, diff_lambda_k2)]
        sub_gain = diff_subln_gain[j].reshape(1, LANES)
        w_in_b = w_in[i].astype(BF16)
        w_out_b = w_out[i].astype(BF16)

        q, k, v, g, kb, vb = _in_proj(xp, w_in_b, tm)
        shp = (b, s, d)
        o = _prompt_mixer(kind, i, q.reshape(shp), k.reshape(shp), kb.reshape(shp), vb.reshape(shp),
                          lam_vecs, sub_gain)
        xp = _out_proj(o.reshape(b * s, d), g, xp, w_out_b, ln_gain[i], ln_bias[i], alpha, tm)
        kp_rows.append(k.reshape(shp))
        vp_rows.append(v.reshape(shp))

        qs, ks, vs, gs, _, _ = _in_proj(xs, w_in_b, db * ds)
        sshp = (db, ds, d)
        os_ = _sample_mixer(kind, i, page_table, qs.reshape(sshp), ks.reshape(sshp), vs.reshape(sshp),
                            cache_k, cache_v, lam_vecs, sub_gain)
        xs = _out_proj(os_.reshape(db * ds, d), gs, xs, w_out_b, ln_gain[i], ln_bias[i], alpha, db * ds)
        ks_rows.append(ks.reshape(sshp))
        vs_rows.append(vs.reshape(sshp))
    return (xp.reshape(b, s, d), xs.reshape(db, ds, d), jnp.stack(kp_rows), jnp.stack(vp_rows),
            jnp.stack(ks_rows), jnp.stack(vs_rows))
```

```python
import functools
import math

import numpy as np
import jax
import jax.numpy as jnp
from jax import lax
from jax.experimental import pallas as pl
from jax.experimental.pallas import tpu as pltpu

F32 = jnp.float32
BF16 = jnp.bfloat16

N_MIXERS = 3
DIFF_HEADS = 8
HEAD_DIM = 64
SB_HEADS = 16
MOBA_HEADS = 16
MOBA_BLOCK = 256
MOBA_TOPK = 3
LN_EPS = 1e-5
QK_SCALE = HEAD_DIM ** -0.5

LANES = 128
BF16_ROWS = 16
V_ROWS = LANES + BF16_ROWS
VMEM_LIMIT = 56 * 1024 * 1024

SB_LOG_ZERO = -104.5
ALIBI_LOG_ZERO = -106.0

KIND_DIFF, KIND_SB, KIND_MOBA = 0, 1, 2
NEG_INF = float("-inf")


def _alibi_slopes(n_heads):
    return np.asarray(2.0 ** (-8.0 * np.arange(1, n_heads + 1) / n_heads), dtype=np.float32)


def _slope_table(n_heads):
    s = _alibi_slopes(n_heads)
    return jnp.asarray(np.stack([s, (1.0 / s).astype(np.float32)]))


def _dot_nt(a, b):
    return lax.dot_general(a, b, (((1,), (1,)), ((), ())), preferred_element_type=F32)


def _diff_lambda(lq1, lk1, lq2, lk2, lam_init):
    a = jnp.exp(jnp.sum(lq1[...] * lk1[...], axis=-1, keepdims=True))
    b = jnp.exp(jnp.sum(lq2[...] * lk2[...], axis=-1, keepdims=True))
    return a - b + lam_init


def _top_select(gate, valid, axis):
    pos = lax.broadcasted_iota(jnp.int32, gate.shape, axis).astype(F32)
    g = jnp.where(valid, gate, NEG_INF)
    sel = jnp.zeros(gate.shape, F32)
    for _ in range(MOBA_TOPK):
        m = jnp.max(g, axis=axis, keepdims=True)
        idx = jnp.min(jnp.where(g == m, pos, 1e9), axis=axis, keepdims=True)
        idx = jnp.where(m > NEG_INF, idx, 1e9)
        pick = pos == idx
        sel = jnp.where(pick, 1.0, sel)
        g = jnp.where(pick, NEG_INF, g)
    return sel


def _softplus(z):
    return jnp.maximum(z, 0.0) + jnp.log1p(jnp.exp(-jnp.abs(z)))


def _split_hi_lo(x):
    hi = x.astype(BF16)
    return hi, (x - hi.astype(F32)).astype(BF16)


def _in_proj_rows_kernel(x_ref, w_ref, q_ref, k_ref, v_ref, g_ref):
    xb = x_ref[...].astype(BF16)
    d = q_ref.shape[-1]
    for j, o_ref in enumerate((q_ref, k_ref, v_ref, g_ref)):
        o_ref[...] = jnp.dot(xb, w_ref[:, j * d:(j + 1) * d], preferred_element_type=F32)


def _in_proj_rows(x2d, w_bf16, tm):
    n, d = x2d.shape
    row = pl.BlockSpec((tm, d), lambda i: (i, 0))
    return pl.pallas_call(
        _in_proj_rows_kernel,
        grid=(n // tm,),
        in_specs=[row, pl.BlockSpec((d, 4 * d), lambda i: (0, 0))],
        out_specs=[row] * 4,
        out_shape=[jax.ShapeDtypeStruct((n, d), F32)] * 4,
        compiler_params=pltpu.CompilerParams(vmem_limit_bytes=VMEM_LIMIT),
        name="in_proj_rows",
    )(x2d, w_bf16)


def _in_proj_prompt_kernel(x_ref, wqt_ref, wvt_ref, w_ref, qt_ref, k_ref, kb_ref, v_ref, vt_ref, g_ref):
    xb = x_ref[...].astype(BF16)
    d = k_ref.shape[-1]
    qt = _dot_nt(wqt_ref[...], xb)
    vt = _dot_nt(wvt_ref[...], xb).astype(BF16)
    ones = jnp.ones((V_ROWS - LANES, vt.shape[1]), BF16)
    for gi in range(d // LANES):
        qt_ref[gi] = qt[gi * LANES:(gi + 1) * LANES, :]
        vt_ref[gi, 0:LANES, :] = vt[gi * LANES:(gi + 1) * LANES, :]
        vt_ref[gi, LANES:V_ROWS, :] = ones
    k = jnp.dot(xb, w_ref[:, 0:d], preferred_element_type=F32)
    k_ref[...] = k
    kb_ref[...] = k.astype(BF16)
    v_ref[...] = jnp.dot(xb, w_ref[:, d:2 * d], preferred_element_type=F32)
    g_ref[...] = jnp.dot(xb, w_ref[:, 2 * d:3 * d], preferred_element_type=F32)


def _in_proj_prompt(x2d, wqt, wvt, w_kvg, tm):
    n, d = x2d.shape
    groups = d // LANES
    row = pl.BlockSpec((tm, d), lambda i: (i, 0))
    sq = pl.BlockSpec((d, d), lambda i: (0, 0))
    return pl.pallas_call(
        _in_proj_prompt_kernel,
        grid=(n // tm,),
        in_specs=[row, sq, sq, pl.BlockSpec((d, 3 * d), lambda i: (0, 0))],
        out_specs=[pl.BlockSpec((groups, LANES, tm), lambda i: (0, 0, i)), row, row, row,
                   pl.BlockSpec((groups, V_ROWS, tm), lambda i: (0, 0, i)), row],
        out_shape=[jax.ShapeDtypeStruct((groups, LANES, n), F32),
                   jax.ShapeDtypeStruct((n, d), F32), jax.ShapeDtypeStruct((n, d), BF16),
                   jax.ShapeDtypeStruct((n, d), F32),
                   jax.ShapeDtypeStruct((groups, V_ROWS, n), BF16),
                   jax.ShapeDtypeStruct((n, d), F32)],
        compiler_params=pltpu.CompilerParams(vmem_limit_bytes=VMEM_LIMIT),
        name="in_proj_prompt",
    )(x2d, wqt, wvt, w_kvg)


def _out_proj_kernel(o_ref, g_ref, x_ref, w_ref, gain_ref, bias_ref, y_ref, *, alpha):
    gt = g_ref[...]
    og = o_ref[...] * (gt / (1.0 + jnp.exp(-gt)))
    y = jnp.dot(og.astype(BF16), w_ref[...], preferred_element_type=F32)
    z = alpha * x_ref[...] + y
    mu = jnp.mean(z, axis=-1, keepdims=True)
    zc = z - mu
    var = jnp.mean(zc * zc, axis=-1, keepdims=True)
    y_ref[...] = zc * lax.rsqrt(var + LN_EPS) * gain_ref[...] + bias_ref[...]


def _out_proj(o2d, g2d, x2d, w_bf16, gain, bias, alpha, tm):
    n, d = x2d.shape
    row = pl.BlockSpec((tm, d), lambda i: (i, 0))
    vec = pl.BlockSpec((1, d), lambda i: (0, 0))
    return pl.pallas_call(
        functools.partial(_out_proj_kernel, alpha=alpha),
        grid=(n // tm,),
        in_specs=[row, row, row, pl.BlockSpec((d, d), lambda i: (0, 0)), vec, vec],
        out_specs=row,
        out_shape=jax.ShapeDtypeStruct((n, d), F32),
        compiler_params=pltpu.CompilerParams(vmem_limit_bytes=VMEM_LIMIT),
        name="out_proj_ln",
    )(o2d, g2d, x2d, w_bf16, gain.reshape(1, d), bias.reshape(1, d))


def _split_maps_t(qt):
    row = lax.broadcasted_iota(jnp.int32, qt.shape, 0)
    return [jnp.where(row < HEAD_DIM, qt, 0.0), jnp.where(row >= HEAD_DIM, qt, 0.0)]


def _key_abs_max(k_ref, seq, chunk):
    def body(it, acc):
        kc = k_ref[pl.ds(pl.multiple_of(it * chunk, chunk), chunk), :].astype(F32)
        return jnp.maximum(acc, jnp.max(jnp.abs(kc), axis=0, keepdims=True))

    kmax = lax.fori_loop(0, seq // chunk, body, jnp.zeros((1, LANES), F32))
    return jnp.broadcast_to(kmax, (BF16_ROWS, LANES)).astype(BF16)


def _score_bound(kabs, qm_f32):
    return jnp.dot(kabs, jnp.abs(qm_f32).astype(BF16), preferred_element_type=F32)[0:1]


def _first_live_tile(excess, slope_inv, q0, t, i):
    e = jnp.max(excess, axis=-1, keepdims=True)
    lim = ((ALIBI_LOG_ZERO - e) * slope_inv + (q0 - t + 1).astype(F32)) * (1.0 / t)
    j_lo = jnp.clip(jnp.ceil(lim), 0.0, i.astype(F32)).astype(jnp.int32)
    return jnp.max(j_lo)


def _softmax_tile(u, shift, m_prev, vt, acc_ref, first):
    m_tile = jnp.max(u, axis=0, keepdims=True) + shift
    m_new = m_tile if first else jnp.maximum(m_prev, m_tile)
    p = jnp.exp(u - (m_new - shift)).astype(BF16)
    pv = jnp.dot(vt, p, preferred_element_type=F32)
    if first:
        acc_ref[...] = pv
    else:
        acc_ref[...] = jnp.exp(m_prev - m_new) * acc_ref[...] + pv
    return m_new


def _diff_prompt_kernel(slope_ref, qt_ref, k_ref, vt_ref, lq1, lk1, lq2, lk2, gain_ref, o_ref,
                        base_sc, kabs_sc, acc_sc, *, t, seq, lam_init):
    g = pl.program_id(1)
    i = pl.program_id(2)
    slope = slope_ref[0, g]

    @pl.when(i == 0)
    def _():
        base_sc[...] = slope * lax.broadcasted_iota(jnp.int32, (t, t), 0).astype(F32)
        kabs_sc[...] = _key_abs_max(k_ref, seq, t)

    qf = _split_maps_t(qt_ref[...] * QK_SCALE)
    qm = [x.astype(BF16) for x in qf]
    q0 = i * t

    def tile(j, m_prev, first):
        k0 = pl.multiple_of(j * t, t)
        kt = k_ref[pl.ds(k0, t), :]
        vt = vt_ref[:, pl.ds(k0, t)]
        shift = slope * (jnp.zeros((1, t), jnp.int32) + (k0 - q0)).astype(F32)
        m_new = []
        for c in range(2):
            u = jnp.dot(kt, qm[c], preferred_element_type=F32) + base_sc[...]
            if first:
                krow = lax.broadcasted_iota(jnp.int32, (t, t), 0)
                qcol = lax.broadcasted_iota(jnp.int32, (t, t), 1)
                u = jnp.where(krow <= qcol, u, NEG_INF)
            m_new.append(_softmax_tile(u, shift, None if first else m_prev[c], vt, acc_sc.at[c], first))
        return tuple(m_new)

    m = tile(i, None, True)
    kabs = kabs_sc[...]
    excess = jnp.maximum(_score_bound(kabs, qf[0]) - m[0], _score_bound(kabs, qf[1]) - m[1])
    j_lo = _first_live_tile(excess, slope_ref[1, g], q0, t, i)
    lax.fori_loop(j_lo, i, lambda j, mm: tile(j, mm, False), m)

    lam = _diff_lambda(lq1, lk1, lq2, lk2, lam_init)
    o = (acc_sc[0, 0:LANES, :] / acc_sc[0, LANES:LANES + 1, :]
         - lam * (acc_sc[1, 0:LANES, :] / acc_sc[1, LANES:LANES + 1, :]))
    ms = jnp.mean(o * o, axis=0, keepdims=True)
    o_ref[...] = (o * lax.rsqrt(ms + LN_EPS)).T * gain_ref[...] * (1.0 - lam_init)


def _moba_prompt_kernel(slope_ref, qt_ref, k_ref, vt_ref, mean_ref, o_ref,
                        base_sc, kabs_sc, acc_sc, sel_sc, *, t, seq):
    g = pl.program_id(1)
    i = pl.program_id(2)
    slopes = [slope_ref[0, 2 * g], slope_ref[0, 2 * g + 1]]

    @pl.when(i == 0)
    def _():
        key_row = lax.broadcasted_iota(jnp.int32, (t, t), 0).astype(F32)
        for h in range(2):
            base_sc[h] = slopes[h] * key_row
        kabs_sc[...] = _key_abs_max(k_ref, seq, t)

    qf = _split_maps_t(qt_ref[...])
    qs = [x * QK_SCALE for x in qf]
    qm = [x.astype(BF16) for x in qs]
    means = mean_ref[...]
    blk = lax.broadcasted_iota(jnp.int32, (means.shape[0], t), 0)
    for h in range(2):
        gate = jnp.dot(means, qf[h], precision=lax.Precision.HIGHEST, preferred_element_type=F32)
        sel_sc[h] = _top_select(gate, blk < i, axis=0)
    q0 = i * t

    def tile(j, m_prev, own):
        k0 = pl.multiple_of(j * t, t)
        kt = k_ref[pl.ds(k0, t), :]
        vt = vt_ref[:, pl.ds(k0, t)]
        rel = (jnp.zeros((1, t), jnp.int32) + (k0 - q0)).astype(F32)
        m_new = []
        for h in range(2):
            u = jnp.dot(kt, qm[h], preferred_element_type=F32) + base_sc[h]
            if own:
                krow = lax.broadcasted_iota(jnp.int32, (t, t), 0)
                qcol = lax.broadcasted_iota(jnp.int32, (t, t), 1)
                u = jnp.where(krow <= qcol, u, NEG_INF)
            else:
                u = jnp.where(sel_sc[h, pl.ds(j, 1), :] > 0.0, u, NEG_INF)
            m_new.append(_softmax_tile(u, slopes[h] * rel, None if own else m_prev[h], vt,
                                       acc_sc.at[h], own))
        return tuple(m_new)

    m = tile(i, None, True)
    kabs = kabs_sc[...]
    j_lo = jnp.minimum(
        _first_live_tile(_score_bound(kabs, qs[0]) - m[0], slope_ref[1, 2 * g], q0, t, i),
        _first_live_tile(_score_bound(kabs, qs[1]) - m[1], slope_ref[1, 2 * g + 1], q0, t, i))
    lax.fori_loop(j_lo, i, lambda j, mm: tile(j, mm, False), m)

    row = lax.broadcasted_iota(jnp.int32, (LANES, t), 0)
    o = jnp.where(row < HEAD_DIM, acc_sc[0, 0:LANES, :] / acc_sc[0, LANES:LANES + 1, :],
                  acc_sc[1, 0:LANES, :] / acc_sc[1, LANES:LANES + 1, :])
    o_ref[...] = o.T


def _suffix_matrix_t(tk):
    s = lax.broadcasted_iota(jnp.int32, (tk + BF16_ROWS, 2 * tk), 0)
    j = lax.broadcasted_iota(jnp.int32, (tk + BF16_ROWS, 2 * tk), 1)
    return jnp.where(s >= tk, 1.0, jnp.where((j & (tk - 1)) > s, 1.0, 0.0)).astype(BF16)


def _sb_prompt_kernel(qt_ref, k_ref, vt_ref, o_ref, acc_sc, *, tq, tk):
    i = pl.program_id(2)
    qm = [x.astype(BF16) for x in _split_maps_t(qt_ref[...] * QK_SCALE)]
    acc_sc[...] = jnp.zeros(acc_sc.shape, F32)
    qpos = i * tq + lax.broadcasted_iota(jnp.int32, (tk, tq), 1)
    krow = lax.broadcasted_iota(jnp.int32, (tk, tq), 0)
    tmat = _suffix_matrix_t(tk)

    def chunk(j, c_prev):
        k0 = pl.multiple_of(j * tk, tk)
        kt = k_ref[pl.ds(k0, tk), :]
        vt = vt_ref[0:LANES, pl.ds(k0, tk)]
        past = (k0 + krow) < qpos
        c_new = []
        for h in range(2):
            z = jnp.dot(kt, qm[h], preferred_element_type=F32)
            lk = jnp.where(past, -_softplus(z), 0.0)
            hi, lo = _split_hi_lo(lk)
            r = jnp.dot(tmat, jnp.concatenate([hi, lo], axis=0), preferred_element_type=F32)
            later = c_prev[h] + r[0:tk]
            a = jnp.where(past, jnp.exp(z + lk + later), 0.0)
            acc_sc[h] = acc_sc[h] + jnp.dot(vt, a.astype(BF16), preferred_element_type=F32)
            c_new.append(c_prev[h] + r[tk:tk + 1])
        return tuple(c_new)

    def cond(carry):
        j, live, _, _ = carry
        return jnp.logical_and(j >= 0, live > 0)

    def body(carry):
        j, _, c0, c1 = carry
        c0, c1 = chunk(j, (c0, c1))
        live = jnp.max(jnp.maximum(c0, c1)) > SB_LOG_ZERO
        return j - 1, live.astype(jnp.int32), c0, c1

    r = tq // tk
    zero = jnp.zeros((1, tq), F32)
    lax.while_loop(cond, body, (i * r + r - 1, jnp.int32(1), zero, zero))
    row = lax.broadcasted_iota(jnp.int32, (LANES, tq), 0)
    o_ref[...] = jnp.where(row < HEAD_DIM, acc_sc[0], acc_sc[1]).T


def _block_mean_kernel(k_ref, o_ref, *, per_step):
    k = k_ref[...]
    o_ref[...] = jnp.mean(k.reshape(per_step, MOBA_BLOCK, k.shape[-1]), axis=1)


def _block_means(k3d):
    b, s, d = k3d.shape
    nb = s // MOBA_BLOCK
    per_step = 8 if nb % 8 == 0 else nb
    return pl.pallas_call(
        functools.partial(_block_mean_kernel, per_step=per_step),
        grid=(b, nb // per_step),
        in_specs=[pl.BlockSpec((None, per_step * MOBA_BLOCK, d), lambda bi, n: (bi, n, 0))],
        out_specs=pl.BlockSpec((None, per_step, d), lambda bi, n: (bi, n, 0)),
        out_shape=jax.ShapeDtypeStruct((b, nb, d), F32),
        compiler_params=pltpu.CompilerParams(vmem_limit_bytes=VMEM_LIMIT),
        name="moba_block_means",
    )(k3d)


def _prompt_mixer(kind, layer, b, s, qt, k, kb, vt, lam_vecs, sub_gain):
    d = k.shape[-1]
    groups = d // LANES
    smem = pl.BlockSpec(memory_space=pltpu.SMEM)
    k_spec = pl.BlockSpec((None, s, LANES), lambda bi, g, i: (bi, 0, g))
    vt_spec = pl.BlockSpec((None, V_ROWS, s), lambda bi, g, i: (g, 0, bi))

    def q_spec(t):
        return pl.BlockSpec((None, LANES, t), lambda bi, g, i: (g, 0, bi * (s // t) + i))

    def o_spec(t):
        return pl.BlockSpec((None, t, LANES), lambda bi, g, i: (bi, i, g))

    params = pltpu.CompilerParams(vmem_limit_bytes=VMEM_LIMIT,
                                  dimension_semantics=("arbitrary", "arbitrary", "arbitrary"))
    out_shape = jax.ShapeDtypeStruct((b, s, d), F32)
    if kind == KIND_DIFF:
        t = min(512, s)
        lam_init = 0.8 - 0.6 * math.exp(-0.3 * layer)
        vec = pl.BlockSpec((1, HEAD_DIM), lambda bi, g, i: (0, 0))
        return pl.pallas_call(
            functools.partial(_diff_prompt_kernel, t=t, seq=s, lam_init=lam_init),
            grid=(b, groups, s // t),
            in_specs=[smem, q_spec(t), k_spec, vt_spec, vec, vec, vec, vec,
                      pl.BlockSpec((1, LANES), lambda bi, g, i: (0, 0))],
            out_specs=o_spec(t),
            out_shape=out_shape,
            scratch_shapes=[pltpu.VMEM((t, t), F32), pltpu.VMEM((BF16_ROWS, LANES), BF16),
                            pltpu.VMEM((2, V_ROWS, t), F32)],
            compiler_params=params,
            name="diff_prompt",
        )(_slope_table(DIFF_HEADS), qt, kb, vt, *lam_vecs, sub_gain)
    if kind == KIND_SB:
        tq, tk = min(256, s), 128
        return pl.pallas_call(
            functools.partial(_sb_prompt_kernel, tq=tq, tk=tk),
            grid=(b, groups, s // tq),
            in_specs=[q_spec(tq), k_spec, vt_spec],
            out_specs=o_spec(tq),
            out_shape=out_shape,
            scratch_shapes=[pltpu.VMEM((2, LANES, tq), F32)],
            compiler_params=params,
            name="sb_prompt",
        )(qt, kb, vt)
    t = MOBA_BLOCK
    nb = s // t
    means = _block_means(k)
    return pl.pallas_call(
        functools.partial(_moba_prompt_kernel, t=t, seq=s),
        grid=(b, groups, nb),
        in_specs=[smem, q_spec(t), k_spec, vt_spec,
                  pl.BlockSpec((None, nb, LANES), lambda bi, g, i: (bi, 0, g))],
        out_specs=o_spec(t),
        out_shape=out_shape,
        scratch_shapes=[pltpu.VMEM((2, t, t), F32), pltpu.VMEM((BF16_ROWS, LANES), BF16),
                        pltpu.VMEM((2, V_ROWS, t), F32), pltpu.VMEM((2, nb, t), F32)],
        compiler_params=params,
        name="moba_prompt",
    )(_slope_table(MOBA_HEADS), qt, kb, vt, means)


REPL = 16


def _row_segment(r, kind):
    rep = r % REPL
    if kind == KIND_DIFF:
        return 2 * (rep % DIFF_HEADS) + rep // DIFF_HEADS
    return rep


def _suffix_matrix(tk):
    j = lax.broadcasted_iota(jnp.int32, (2 * tk, tk), 0)
    s = lax.broadcasted_iota(jnp.int32, (2 * tk, tk), 1)
    return jnp.where((j & (tk - 1)) > s, 1.0, 0.0).astype(BF16)


def _sample_kernel(pt_ref, qrep_ref, knew_ref, vnew_ref, slope_ref, lq1, lk1, lq2, lk2, gain_ref,
                   *rest, kind, pages, nkc, past, n_new, lam_init):
    k_refs = rest[:pages]
    v_refs = rest[pages:2 * pages]
    o_ref = rest[2 * pages]
    wq_sc, wqf_sc, s_sc, p_sc, acc_sc, gate_sc = rest[2 * pages + 1:]
    del pt_ref
    step = pl.program_id(1)
    page = k_refs[0].shape[0]
    rows, d = wqf_sc.shape
    width = s_sc.shape[1]
    out_rows = rows // 2 if kind == KIND_DIFF else rows
    row_id = lax.broadcasted_iota(jnp.int32, (rows, 1), 0)
    qpos = past + row_id // REPL

    @pl.when(step == 0)
    def _():
        col = lax.broadcasted_iota(jnp.int32, (1, d), 1)
        wq = jnp.where(col // HEAD_DIM == _row_segment(row_id, kind), qrep_ref[...], 0.0)
        wqf_sc[...] = wq
        wq_sc[...] = (wq * QK_SCALE).astype(BF16)
        gate_sc[...] = jnp.zeros(gate_sc.shape, F32)

    @pl.when(step < nkc)
    def _():
        for pp in range(0, pages, 2):
            k0 = k_refs[pp][...]
            k1 = k_refs[pp + 1][...]
            kk = jnp.concatenate([k0.astype(BF16), k1.astype(BF16)], axis=0)
            col0 = pl.multiple_of((step * pages + pp) * page, 2 * page)
            s_sc[:, pl.ds(col0, 2 * page)] = _dot_nt(wq_sc[...], kk)
            if kind == KIND_MOBA:
                mean = (jnp.sum(k0, axis=0, keepdims=True) + jnp.sum(k1, axis=0, keepdims=True)) * (1.0 / MOBA_BLOCK)
                gcol = jnp.sum(wqf_sc[...] * mean, axis=-1, keepdims=True)
                lane = lax.broadcasted_iota(jnp.int32, gate_sc.shape, 1)
                gate_sc[...] = jnp.where(lane == (step * pages + pp) // 2, gcol, gate_sc[...])

    @pl.when(step == nkc)
    def _():
        s_sc[:, past:past + page] = _dot_nt(wq_sc[...], knew_ref[...])
        col = lax.broadcasted_iota(jnp.int32, (1, width), 1)
        if kind == KIND_SB:
            t2 = _suffix_matrix(page)
            n_chunks = width // page

            def body(it, c_prev):
                c0 = pl.multiple_of((n_chunks - 1 - it) * page, page)
                z = s_sc[:, pl.ds(c0, page)]
                colpos = c0 + lax.broadcasted_iota(jnp.int32, (1, page), 1)
                prior = colpos < qpos
                lk = jnp.where(prior, -_softplus(z), 0.0)
                hi, lo = _split_hi_lo(lk)
                later = c_prev + jnp.dot(jnp.concatenate([hi, lo], axis=1), t2, preferred_element_type=F32)
                p_sc[:, pl.ds(c0, page)] = jnp.where(prior, jnp.exp(z + lk + later), 0.0)
                return c_prev + jnp.sum(lk, axis=-1, keepdims=True)

            lax.fori_loop(0, n_chunks, body, jnp.zeros((rows, 1), F32))
        else:
            dist = (qpos - col).astype(F32)
            ok = dist >= 0.0
            if kind == KIND_MOBA:
                own = past // MOBA_BLOCK
                lane = lax.broadcasted_iota(jnp.int32, gate_sc.shape, 1)
                sel = _top_select(gate_sc[...], lane < own, axis=1)
                sel = jnp.where(lane == own, 1.0, sel)
                blk_row = lax.broadcasted_iota(jnp.int32, (gate_sc.shape[1], 1), 0)
                expand = jnp.where(col // MOBA_BLOCK == blk_row, 1.0, 0.0).astype(BF16)
                picked = jnp.dot(sel.astype(BF16), expand, preferred_element_type=F32)
                ok = jnp.logical_and(ok, picked > 0.5)
            sb = jnp.where(ok, s_sc[...] - slope_ref[...] * dist, NEG_INF)
            e = jnp.exp(sb - jnp.max(sb, axis=-1, keepdims=True))
            p = e / jnp.sum(e, axis=-1, keepdims=True)
            if kind == KIND_DIFF:
                lam = _diff_lambda(lq1, lk1, lq2, lk2, lam_init)
                half = REPL // 2
                for qi in range(n_new):
                    p_sc[qi * half:(qi + 1) * half, :] = (
                        p[qi * REPL:qi * REPL + half] - lam * p[qi * REPL + half:(qi + 1) * REPL])
            else:
                p_sc[...] = p
        acc_sc[0:out_rows, :] = jnp.dot(p_sc[0:out_rows, past:past + page].astype(BF16), vnew_ref[...],
                                        preferred_element_type=F32)

    @pl.when(step >= nkc)
    def _():
        for pp in range(0, pages, 2):
            vv = jnp.concatenate([v_refs[pp][...].astype(BF16), v_refs[pp + 1][...].astype(BF16)], axis=0)
            col0 = pl.multiple_of(((step - nkc) * pages + pp) * page, 2 * page)
            w = p_sc[0:out_rows, pl.ds(col0, 2 * page)].astype(BF16)
            acc_sc[0:out_rows, :] = acc_sc[0:out_rows, :] + jnp.dot(w, vv, preferred_element_type=F32)

    @pl.when(step == 2 * nkc - 1)
    def _():
        per_q = out_rows // n_new
        seg_w = LANES if kind == KIND_DIFF else HEAD_DIM
        r = lax.broadcasted_iota(jnp.int32, (per_q, 1), 0)
        col = lax.broadcasted_iota(jnp.int32, (1, d), 1)
        own_cols = col // seg_w == r
        for qi in range(n_new):
            a = jnp.where(own_cols, acc_sc[qi * per_q:(qi + 1) * per_q, :], 0.0)
            if kind == KIND_DIFF:
                ms = jnp.sum(a * a, axis=-1, keepdims=True) * (1.0 / LANES)
                a = a * lax.rsqrt(ms + LN_EPS) * gain_ref[...] * (1.0 - lam_init)
            o_ref[qi:qi + 1, :] = jnp.sum(a, axis=0, keepdims=True)


def _sample_mixer(kind, layer, page_table, q, k_new, v_new, cache_k, cache_v, lam_vecs, sub_gain):
    db, n_new, d = q.shape
    n_pages = page_table.shape[1]
    page = cache_k.shape[2]
    past = n_pages * page
    pages = 8 if n_pages % 8 == 0 else 2
    nkc = n_pages // pages
    rows = n_new * REPL
    width = past + page
    assert MOBA_BLOCK == 2 * page and past % MOBA_BLOCK == 0 and past // MOBA_BLOCK < LANES
    assert n_pages % pages == 0 and n_new <= page and rows % 16 == 0
    lam_init = 0.8 - 0.6 * math.exp(-0.3 * layer)

    qrep = jnp.repeat(q, REPL, axis=1)
    pad = ((0, 0), (0, page - n_new), (0, 0))
    knew = jnp.pad(k_new, pad).astype(BF16)
    vnew = jnp.pad(v_new, pad).astype(BF16)
    rep = np.arange(rows) % REPL
    if kind == KIND_DIFF:
        slopes = _alibi_slopes(DIFF_HEADS)[rep % DIFF_HEADS]
    elif kind == KIND_MOBA:
        slopes = _alibi_slopes(MOBA_HEADS)[rep]
    else:
        slopes = np.zeros((rows,), np.float32)
    slope_rows = jnp.asarray(slopes.reshape(rows, 1))
    gain_full = jnp.tile(sub_gain, (1, d // LANES))

    def per_sample(shape):
        return pl.BlockSpec((None,) + shape, lambda b, s, pt: (b, 0, 0))

    def const(shape):
        return pl.BlockSpec(shape, lambda b, s, pt: (0, 0))

    def k_spec(p):
        return pl.BlockSpec(
            (None, None, page, d),
            lambda b, s, pt: (layer, pt[b * n_pages + jnp.minimum(s, nkc - 1) * pages + p], 0, 0))

    def v_spec(p):
        return pl.BlockSpec(
            (None, None, page, d),
            lambda b, s, pt: (layer, pt[b * n_pages + jnp.maximum(s - nkc, 0) * pages + p], 0, 0))

    vec = const((1, HEAD_DIM))
    grid_spec = pltpu.PrefetchScalarGridSpec(
        num_scalar_prefetch=1,
        grid=(db, 2 * nkc),
        in_specs=[per_sample((rows, d)), per_sample((page, d)), per_sample((page, d)),
                  const((rows, 1)), vec, vec, vec, vec, const((1, d))]
                 + [k_spec(p) for p in range(pages)] + [v_spec(p) for p in range(pages)],
        out_specs=per_sample((n_new, d)),
        scratch_shapes=[pltpu.VMEM((rows, d), BF16), pltpu.VMEM((rows, d), F32),
                        pltpu.VMEM((rows, width), F32), pltpu.VMEM((rows, width), F32),
                        pltpu.VMEM((rows, d), F32), pltpu.VMEM((rows, LANES), F32)],
    )
    return pl.pallas_call(
        functools.partial(_sample_kernel, kind=kind, pages=pages, nkc=nkc, past=past,
                          n_new=n_new, lam_init=lam_init),
        grid_spec=grid_spec,
        out_shape=jax.ShapeDtypeStruct((db, n_new, d), F32),
        compiler_params=pltpu.CompilerParams(vmem_limit_bytes=VMEM_LIMIT),
        name=("diff_sample", "sb_sample", "moba_sample")[kind],
    )(page_table.reshape(-1), qrep, knew, vnew, slope_rows, *lam_vecs, gain_full,
      *([cache_k] * pages), *([cache_v] * pages))


def kernel(x_prompt, x_sample, cache_k, cache_v, page_table, w_in, w_out, ln_gain, ln_bias,
           diff_lambda_q1, diff_lambda_k1, diff_lambda_q2, diff_lambda_k2, diff_subln_gain):
    depth = w_in.shape[0]
    b, s, d = x_prompt.shape
    db, ds, _ = x_sample.shape
    alpha = (2 * depth) ** 0.25
    tm_in = min(256, b * s)
    tm_out = min(512, b * s)
    xp = x_prompt.reshape(b * s, d)
    xs = x_sample.reshape(db * ds, d)
    kp_rows, vp_rows, ks_rows, vs_rows = [], [], [], []
    for i in range(depth):
        kind = i % N_MIXERS
        j = i // N_MIXERS
        lam_vecs = [v[j].reshape(1, HEAD_DIM) for v in
                    (diff_lambda_q1, diff_lambda_k1, diff_lambda_q2, diff_lambda_k2)]
        sub_gain = diff_subln_gain[j].reshape(1, LANES)
        w_in_b = w_in[i].astype(BF16)
        w_out_b = w_out[i].astype(BF16)

        qt, k, kb, v, vt, g = _in_proj_prompt(xp, w_in_b[:, 0:d].T, w_in_b[:, 2 * d:3 * d].T,
                                              w_in_b[:, d:4 * d], tm_in)
        shp = (b, s, d)
        o = _prompt_mixer(kind, i, b, s, qt, k.reshape(shp), kb.reshape(shp), vt, lam_vecs, sub_gain)
        xp = _out_proj(o.reshape(b * s, d), g, xp, w_out_b, ln_gain[i], ln_bias[i], alpha, tm_out)
        kp_rows.append(k.reshape(shp))
        vp_rows.append(v.reshape(shp))

        qs, ks, vs, gs = _in_proj_rows(xs, w_in_b, db * ds)
        sshp = (db, ds, d)
        os_ = _sample_mixer(kind, i, page_table, qs.reshape(sshp), ks.reshape(sshp), vs.reshape(sshp),
                            cache_k, cache_v, lam_vecs, sub_gain)
        xs = _out_proj(os_.reshape(db * ds, d), gs, xs, w_out_b, ln_gain[i], ln_bias[i], alpha, db * ds)
        ks_rows.append(ks.reshape(sshp))
        vs_rows.append(vs.reshape(sshp))
    return (xp.reshape(b, s, d), xs.reshape(db, ds, d), jnp.stack(kp_rows), jnp.stack(vp_rows),
            jnp.stack(ks_rows), jnp.stack(vs_rows))
```

```python
import functools
import math

import ml_dtypes
import numpy as np
import jax
import jax.numpy as jnp
from jax import lax
from jax.experimental import pallas as pl
from jax.experimental.pallas import tpu as pltpu

F32 = jnp.float32
BF16 = jnp.bfloat16

N_MIXERS = 3
DIFF_HEADS = 8
HEAD_DIM = 64
SB_HEADS = 16
MOBA_HEADS = 16
MOBA_BLOCK = 256
MOBA_TOPK = 3
LN_EPS = 1e-5
QK_SCALE = HEAD_DIM ** -0.5
LOG2E = 1.4426950408889634
LN2 = 0.6931471805599453

LANES = 128
BF16_ROWS = 16
V_ROWS = LANES + BF16_ROWS
VMEM_LIMIT = 56 * 1024 * 1024

SB_LOG_ZERO = -104.5
ALIBI_LOG_ZERO = -106.0

KIND_DIFF, KIND_SB, KIND_MOBA = 0, 1, 2
NEG_INF = float("-inf")


def _alibi_slopes(n_heads):
    return np.asarray(2.0 ** (-8.0 * np.arange(1, n_heads + 1) / n_heads), dtype=np.float32)


def _slope_table(n_heads):
    s = _alibi_slopes(n_heads)
    rest = (s.astype(np.float64) * LOG2E).astype(np.float32)
    parts = []
    for _ in range(3):
        p = rest.astype(ml_dtypes.bfloat16).astype(np.float32)
        parts.append(p)
        rest = rest - p
    return jnp.asarray(np.stack([s, (1.0 / s).astype(np.float32)] + parts))


def _alibi_key_codes(tk):
    r = lax.broadcasted_iota(jnp.int32, (tk, LANES), 0)
    lane = lax.broadcasted_iota(jnp.int32, (tk, LANES), 1)
    code = jnp.where(lane < 3, r - (r & 1), jnp.where(lane < 6, r & 1, 0))
    return code.astype(F32).astype(BF16)


def _alibi_query_rows(parts, tq):
    row = lax.broadcasted_iota(jnp.int32, (LANES, tq), 0)
    val = jnp.where(row % 3 == 0, parts[0], jnp.where(row % 3 == 1, parts[1], parts[2]))
    return jnp.where(row < 6, val, 0.0).astype(BF16)


def _dot_nt(a, b):
    return lax.dot_general(a, b, (((1,), (1,)), ((), ())), preferred_element_type=F32)


def _diff_lambda(lq1, lk1, lq2, lk2, lam_init):
    a = jnp.exp(jnp.sum(lq1[...] * lk1[...], axis=-1, keepdims=True))
    b = jnp.exp(jnp.sum(lq2[...] * lk2[...], axis=-1, keepdims=True))
    return a - b + lam_init


def _top_select(gate, valid, axis):
    pos = lax.broadcasted_iota(jnp.int32, gate.shape, axis).astype(F32)
    g = jnp.where(valid, gate, NEG_INF)
    sel = jnp.zeros(gate.shape, F32)
    for _ in range(MOBA_TOPK):
        m = jnp.max(g, axis=axis, keepdims=True)
        idx = jnp.min(jnp.where(g == m, pos, 1e9), axis=axis, keepdims=True)
        idx = jnp.where(m > NEG_INF, idx, 1e9)
        pick = pos == idx
        sel = jnp.where(pick, 1.0, sel)
        g = jnp.where(pick, NEG_INF, g)
    return sel


def _softplus(z):
    return jnp.maximum(z, 0.0) + jnp.log1p(jnp.exp(-jnp.abs(z)))


def _split_hi_lo(x):
    hi = x.astype(BF16)
    return hi, (x - hi.astype(F32)).astype(BF16)


def _in_proj_rows_kernel(x_ref, w_ref, q_ref, k_ref, v_ref, g_ref):
    xb = x_ref[...].astype(BF16)
    d = q_ref.shape[-1]
    for j, o_ref in enumerate((q_ref, k_ref, v_ref, g_ref)):
        o_ref[...] = jnp.dot(xb, w_ref[:, j * d:(j + 1) * d], preferred_element_type=F32)


def _in_proj_rows(x2d, w_bf16, tm):
    n, d = x2d.shape
    row = pl.BlockSpec((tm, d), lambda i: (i, 0))
    return pl.pallas_call(
        _in_proj_rows_kernel,
        grid=(n // tm,),
        in_specs=[row, pl.BlockSpec((d, 4 * d), lambda i: (0, 0))],
        out_specs=[row] * 4,
        out_shape=[jax.ShapeDtypeStruct((n, d), F32)] * 4,
        compiler_params=pltpu.CompilerParams(vmem_limit_bytes=VMEM_LIMIT),
        name="in_proj_rows",
    )(x2d, w_bf16)


def _in_proj_prompt_kernel(x_ref, wqt_ref, wvt_ref, w_ref, *rest):
    qt_ref, k_ref, kb_ref, v_ref, vt_ref, g_ref = rest[-6:]
    xb = x_ref[...].astype(BF16)
    d = k_ref.shape[-1]
    qt = _dot_nt(wqt_ref[...], xb)
    vt = _dot_nt(wvt_ref[...], xb).astype(BF16)
    ones = jnp.ones((V_ROWS - LANES, vt.shape[1]), BF16)
    for gi in range(d // LANES):
        qt_ref[gi] = qt[gi * LANES:(gi + 1) * LANES, :]
        vt_ref[gi, 0:LANES, :] = vt[gi * LANES:(gi + 1) * LANES, :]
        vt_ref[gi, LANES:V_ROWS, :] = ones
    k = jnp.dot(xb, w_ref[:, 0:d], preferred_element_type=F32)
    k_ref[...] = k
    kb_ref[...] = k.astype(BF16)
    v_ref[...] = jnp.dot(xb, w_ref[:, d:2 * d], preferred_element_type=F32)
    g_ref[...] = jnp.dot(xb, w_ref[:, 2 * d:3 * d], preferred_element_type=F32)


def _in_proj_prompt(x2d, wqt, wvt, w_kvg, tm, layer, depth, k_all, v_all):
    n, d = x2d.shape
    groups = d // LANES
    row = pl.BlockSpec((tm, d), lambda i: (i, 0))
    sq = pl.BlockSpec((d, d), lambda i: (0, 0))
    layer_row = pl.BlockSpec((None, tm, d), lambda i: (layer, i, 0))
    stack = jax.ShapeDtypeStruct((depth, n, d), F32)
    in_specs = [row, sq, sq, pl.BlockSpec((d, 3 * d), lambda i: (0, 0))]
    args = [x2d, wqt, wvt, w_kvg]
    aliases = {}
    if k_all is not None:
        in_specs += [pl.BlockSpec(memory_space=pl.ANY)] * 2
        args += [k_all, v_all]
        aliases = {4: 1, 5: 3}
    return pl.pallas_call(
        _in_proj_prompt_kernel,
        grid=(n // tm,),
        in_specs=in_specs,
        out_specs=[pl.BlockSpec((groups, LANES, tm), lambda i: (0, 0, i)), layer_row, row, layer_row,
                   pl.BlockSpec((groups, V_ROWS, tm), lambda i: (0, 0, i)), row],
        out_shape=[jax.ShapeDtypeStruct((groups, LANES, n), F32), stack, jax.ShapeDtypeStruct((n, d), BF16),
                   stack, jax.ShapeDtypeStruct((groups, V_ROWS, n), BF16), jax.ShapeDtypeStruct((n, d), F32)],
        input_output_aliases=aliases,
        compiler_params=pltpu.CompilerParams(vmem_limit_bytes=VMEM_LIMIT),
        name="in_proj_prompt",
    )(*args)


def _out_proj_kernel(o_ref, g_ref, x_ref, w_ref, gain_ref, bias_ref, y_ref, *, alpha):
    gt = g_ref[...]
    og = o_ref[...] * (gt / (1.0 + jnp.exp(-gt)))
    y = jnp.dot(og.astype(BF16), w_ref[...], preferred_element_type=F32)
    z = alpha * x_ref[...] + y
    mu = jnp.mean(z, axis=-1, keepdims=True)
    zc = z - mu
    var = jnp.mean(zc * zc, axis=-1, keepdims=True)
    y_ref[...] = zc * lax.rsqrt(var + LN_EPS) * gain_ref[...] + bias_ref[...]


def _out_proj(o2d, g2d, x2d, w_bf16, gain, bias, alpha, tm):
    n, d = x2d.shape
    row = pl.BlockSpec((tm, d), lambda i: (i, 0))
    vec = pl.BlockSpec((1, d), lambda i: (0, 0))
    return pl.pallas_call(
        functools.partial(_out_proj_kernel, alpha=alpha),
        grid=(n // tm,),
        in_specs=[row, row, row, pl.BlockSpec((d, d), lambda i: (0, 0)), vec, vec],
        out_specs=row,
        out_shape=jax.ShapeDtypeStruct((n, d), F32),
        compiler_params=pltpu.CompilerParams(vmem_limit_bytes=VMEM_LIMIT),
        name="out_proj_ln",
    )(o2d, g2d, x2d, w_bf16, gain.reshape(1, d), bias.reshape(1, d))


def _split_maps_t(qt):
    row = lax.broadcasted_iota(jnp.int32, qt.shape, 0)
    return [jnp.where(row < HEAD_DIM, qt, 0.0), jnp.where(row >= HEAD_DIM, qt, 0.0)]


def _key_abs_max(k_ref, seq, chunk):
    def body(it, acc):
        kc = k_ref[pl.ds(pl.multiple_of(it * chunk, chunk), chunk), :].astype(F32)
        return jnp.maximum(acc, jnp.max(jnp.abs(kc), axis=0, keepdims=True))

    kmax = lax.fori_loop(0, seq // chunk, body, jnp.zeros((1, LANES), F32))
    return jnp.broadcast_to(kmax, (BF16_ROWS, LANES)).astype(BF16)


def _score_bound(kabs, qm_f32):
    return jnp.dot(kabs, jnp.abs(qm_f32).astype(BF16), preferred_element_type=F32)[0:1]


def _first_live_tile(excess, slope_inv, q0, tk, hi):
    e = jnp.max(excess, axis=-1, keepdims=True) * LN2
    lim = ((ALIBI_LOG_ZERO - e) * slope_inv + (q0 - tk + 1).astype(F32)) * (1.0 / tk)
    j_lo = jnp.clip(jnp.ceil(lim), 0.0, hi.astype(F32)).astype(jnp.int32)
    return jnp.max(j_lo)


def _softmax_tile(u_ref, m_tile, shift, m_prev, vt, acc_ref):
    m_new = jnp.maximum(m_prev, m_tile + shift)
    m_fin = jnp.where(m_new > NEG_INF, m_new, 0.0)
    p = jnp.exp2(u_ref[...] - (m_fin - shift)).astype(BF16)
    pv = jnp.dot(vt, p, preferred_element_type=F32)
    acc_ref[...] = jnp.exp2(m_prev - m_fin) * acc_ref[...] + pv
    return m_new


def _pipelined_sweep(n_diag, first_diag, scores, update, first_live, m_init):
    m = m_init
    mt_cur = scores(first_diag, 0, 0)
    j_lo = None
    last_past = jnp.maximum(first_diag - 1, 0)
    for n in range(n_diag):
        if n + 1 < n_diag:
            mt_nxt = scores(first_diag + n + 1, (n + 1) & 1, n + 1)
        else:
            j_lo = first_live(m, mt_cur, n)
            mt_nxt = scores(jnp.minimum(j_lo, last_past), n_diag & 1, None)
        m = update(first_diag + n, n & 1, mt_cur, m, True)
        mt_cur = mt_nxt

    slot_a = n_diag & 1
    slot_b = slot_a ^ 1

    def body(pair, carry):
        m_prev, mt_a = carry
        a = j_lo + 2 * pair
        b = jnp.minimum(a + 1, last_past)
        mt_b = scores(b, slot_b, None)
        m_mid = update(a, slot_a, mt_a, m_prev, True)
        mt_next = scores(jnp.minimum(a + 2, last_past), slot_a, None)
        return update(b, slot_b, mt_b, m_mid, a + 1 < first_diag), mt_next

    lax.fori_loop(0, (first_diag - j_lo + 1) // 2, body, (m, mt_cur))


def _diff_prompt_kernel(slope_ref, qt_ref, k_ref, vt_ref, lq1, lk1, lq2, lk2, gain_ref, o_ref,
                        kcode_sc, kabs_sc, acc_sc, u_sc, *, t, seq, lam_init):
    g = pl.program_id(1)
    i = pl.program_id(2)
    slope2 = slope_ref[0, g] * LOG2E

    @pl.when(i == 0)
    def _():
        kcode_sc[...] = _alibi_key_codes(t)
        kabs_sc[...] = _key_abs_max(k_ref, seq, t)

    qf = _split_maps_t(qt_ref[...] * (QK_SCALE * LOG2E))
    slope_rows = _alibi_query_rows([slope_ref[2 + e, g] for e in range(3)], t)
    qm = [jnp.concatenate([x.astype(BF16), slope_rows], axis=0) for x in qf]
    q0 = i * t
    acc_sc[...] = jnp.zeros(acc_sc.shape, F32)

    def scores(j, slot, diag_n):
        kt = jnp.concatenate([k_ref[pl.ds(pl.multiple_of(j * t, t), t), :], kcode_sc[...]], axis=1)
        maxima = []
        for c in range(2):
            u = jnp.dot(kt, qm[c], preferred_element_type=F32)
            if diag_n is not None:
                krow = lax.broadcasted_iota(jnp.int32, (t, t), 0)
                qcol = lax.broadcasted_iota(jnp.int32, (t, t), 1)
                u = jnp.where(krow <= qcol, u, NEG_INF)
            u_sc[slot, c] = u
            maxima.append(jnp.max(u, axis=0, keepdims=True))
        return tuple(maxima)

    def update(j, slot, maxima, m_prev, valid):
        k0 = pl.multiple_of(j * t, t)
        vt = vt_ref[:, pl.ds(k0, t)]
        shift = slope2 * (jnp.zeros((1, t), jnp.int32) + (k0 - q0)).astype(F32)
        if valid is not True:
            shift = jnp.where(valid, shift, NEG_INF)
        return tuple(_softmax_tile(u_sc.at[slot, c], maxima[c], shift, m_prev[c], vt, acc_sc.at[c])
                     for c in range(2))

    def first_live(m_prev, maxima, n):
        del m_prev, n
        kabs = kabs_sc[...]
        excess = jnp.maximum(_score_bound(kabs, qf[0]) - maxima[0], _score_bound(kabs, qf[1]) - maxima[1])
        return _first_live_tile(excess, slope_ref[1, g], q0, t, i)

    neg = jnp.full((1, t), NEG_INF, F32)
    _pipelined_sweep(1, i, scores, update, first_live, (neg, neg))

    lam = _diff_lambda(lq1, lk1, lq2, lk2, lam_init)
    o = (acc_sc[0, 0:LANES, :] / acc_sc[0, LANES:LANES + 1, :]
         - lam * (acc_sc[1, 0:LANES, :] / acc_sc[1, LANES:LANES + 1, :]))
    ms = jnp.mean(o * o, axis=0, keepdims=True)
    o_ref[...] = (o * lax.rsqrt(ms + LN_EPS)).T * gain_ref[...] * (1.0 - lam_init)


def _moba_prompt_kernel(slope_ref, qt_ref, k_ref, vt_ref, mean_ref, o_ref,
                        kcode_sc, kabs_sc, acc_sc, sel_sc, u_sc, *, tq, seq):
    tk = MOBA_BLOCK
    n_diag = tq // tk
    g = pl.program_id(1)
    i = pl.program_id(2)
    slopes = [slope_ref[0, 2 * g] * LOG2E, slope_ref[0, 2 * g + 1] * LOG2E]

    @pl.when(i == 0)
    def _():
        kcode_sc[...] = _alibi_key_codes(tk)
        kabs_sc[...] = _key_abs_max(k_ref, seq, tk)

    qf = _split_maps_t(qt_ref[...])
    qs = [x * (QK_SCALE * LOG2E) for x in qf]
    qm = [jnp.concatenate([qs[h].astype(BF16),
                           _alibi_query_rows([slope_ref[2 + e, 2 * g + h] for e in range(3)], tq)], axis=0)
          for h in range(2)]
    means = mean_ref[...]
    first_diag = i * n_diag
    col_blk = lax.broadcasted_iota(jnp.int32, (1, tq), 1) // tk
    blk = lax.broadcasted_iota(jnp.int32, (means.shape[0], tq), 0)
    for h in range(2):
        gate = jnp.dot(means, qf[h], precision=lax.Precision.HIGHEST, preferred_element_type=F32)
        sel_sc[h] = _top_select(gate, blk < first_diag + col_blk, axis=0)
    q0 = i * tq
    acc_sc[...] = jnp.zeros(acc_sc.shape, F32)

    def scores(j, slot, diag_n):
        kt = jnp.concatenate([k_ref[pl.ds(pl.multiple_of(j * tk, tk), tk), :], kcode_sc[...]], axis=1)
        maxima = []
        for h in range(2):
            u = jnp.dot(kt, qm[h], preferred_element_type=F32)
            picked = sel_sc[h, pl.ds(j, 1), :]
            if diag_n is None:
                u = jnp.where(picked > 0.0, u, NEG_INF)
            else:
                allowed = jnp.where(col_blk == diag_n, 1.0, picked)
                krow = lax.broadcasted_iota(jnp.int32, (tk, tq), 0) + diag_n * tk
                qcol = lax.broadcasted_iota(jnp.int32, (tk, tq), 1)
                u = jnp.where(krow <= qcol, jnp.where(allowed > 0.0, u, NEG_INF), NEG_INF)
            u_sc[slot, h] = u
            maxima.append(jnp.max(u, axis=0, keepdims=True))
        return tuple(maxima)

    def update(j, slot, maxima, m_prev, valid):
        k0 = pl.multiple_of(j * tk, tk)
        vt = vt_ref[:, pl.ds(k0, tk)]
        rel = (jnp.zeros((1, tq), jnp.int32) + (k0 - q0)).astype(F32)
        if valid is not True:
            rel = jnp.where(valid, rel, NEG_INF)
        return tuple(_softmax_tile(u_sc.at[slot, h], maxima[h], slopes[h] * rel, m_prev[h], vt, acc_sc.at[h])
                     for h in range(2))

    def first_live(m_prev, maxima, n):
        kabs = kabs_sc[...]
        j_lo = None
        for h in range(2):
            m_h = jnp.maximum(m_prev[h], maxima[h] + slopes[h] * float(n * tk))
            j_h = _first_live_tile(_score_bound(kabs, qs[h]) - m_h, slope_ref[1, 2 * g + h], q0, tk, first_diag)
            j_lo = j_h if j_lo is None else jnp.minimum(j_lo, j_h)
        return j_lo

    neg = jnp.full((1, tq), NEG_INF, F32)
    _pipelined_sweep(n_diag, first_diag, scores, update, first_live, (neg, neg))

    t = tq
    row = lax.broadcasted_iota(jnp.int32, (LANES, t), 0)
    o = jnp.where(row < HEAD_DIM, acc_sc[0, 0:LANES, :] / acc_sc[0, LANES:LANES + 1, :],
                  acc_sc[1, 0:LANES, :] / acc_sc[1, LANES:LANES + 1, :])
    o_ref[...] = o.T


def _suffix_matrix_t(tk):
    s = lax.broadcasted_iota(jnp.int32, (tk + BF16_ROWS, 2 * tk), 0)
    j = lax.broadcasted_iota(jnp.int32, (tk + BF16_ROWS, 2 * tk), 1)
    return jnp.where(s >= tk, 1.0, jnp.where((j & (tk - 1)) > s, 1.0, 0.0)).astype(BF16)


def _sb_prompt_kernel(qt_ref, k_ref, vt_ref, o_ref, acc_sc, *, tq, tk):
    i = pl.program_id(2)
    qm = [x.astype(BF16) for x in _split_maps_t(qt_ref[...] * QK_SCALE)]
    acc_sc[...] = jnp.zeros(acc_sc.shape, F32)
    qpos = i * tq + lax.broadcasted_iota(jnp.int32, (tk, tq), 1)
    krow = lax.broadcasted_iota(jnp.int32, (tk, tq), 0)
    tmat = _suffix_matrix_t(tk)

    def chunk(j, c_prev):
        k0 = pl.multiple_of(j * tk, tk)
        kt = k_ref[pl.ds(k0, tk), :]
        vt = vt_ref[0:LANES, pl.ds(k0, tk)]
        past = (k0 + krow) < qpos
        z = [jnp.dot(kt, qm[h], preferred_element_type=F32) for h in range(2)]
        lk = [jnp.where(past, -_softplus(z[h]), 0.0) for h in range(2)]
        suf = [jnp.dot(tmat, jnp.concatenate(_split_hi_lo(lk[h]), axis=0), preferred_element_type=F32)
               for h in range(2)]
        c_new = []
        for h in range(2):
            later = c_prev[h] + suf[h][0:tk]
            a = jnp.where(past, jnp.exp(z[h] + lk[h] + later), 0.0)
            acc_sc[h] = acc_sc[h] + jnp.dot(vt, a.astype(BF16), preferred_element_type=F32)
            c_new.append(c_prev[h] + suf[h][tk:tk + 1])
        return tuple(c_new)

    def cond(carry):
        j, live, _, _ = carry
        return jnp.logical_and(j >= 0, live > 0)

    def body(carry):
        j, _, c0, c1 = carry
        c0, c1 = chunk(j, (c0, c1))
        live = jnp.max(jnp.maximum(c0, c1)) > SB_LOG_ZERO
        return j - 1, live.astype(jnp.int32), c0, c1

    zero = jnp.zeros((1, tq), F32)
    lax.while_loop(cond, body, (i * (tq // tk) + tq // tk - 1, jnp.int32(1), zero, zero))
    row = lax.broadcasted_iota(jnp.int32, (LANES, tq), 0)
    o_ref[...] = jnp.where(row < HEAD_DIM, acc_sc[0], acc_sc[1]).T


def _block_mean_kernel(k_ref, o_ref, *, per_step):
    k = k_ref[...]
    o_ref[...] = jnp.mean(k.reshape(per_step, MOBA_BLOCK, k.shape[-1]), axis=1)


def _block_means(k_all, layer):
    _, b, s, d = k_all.shape
    nb = s // MOBA_BLOCK
    per_step = 8 if nb % 8 == 0 else nb
    return pl.pallas_call(
        functools.partial(_block_mean_kernel, per_step=per_step),
        grid=(b, nb // per_step),
        in_specs=[pl.BlockSpec((None, None, per_step * MOBA_BLOCK, d), lambda bi, n: (layer, bi, n, 0))],
        out_specs=pl.BlockSpec((None, per_step, d), lambda bi, n: (bi, n, 0)),
        out_shape=jax.ShapeDtypeStruct((b, nb, d), F32),
        compiler_params=pltpu.CompilerParams(vmem_limit_bytes=VMEM_LIMIT),
        name="moba_block_means",
    )(k_all)


def _prompt_mixer(kind, layer, b, s, qt, k, kb, vt, lam_vecs, sub_gain):
    d = k.shape[-1]
    groups = d // LANES
    smem = pl.BlockSpec(memory_space=pltpu.SMEM)
    k_spec = pl.BlockSpec((None, s, LANES), lambda bi, g, i: (bi, 0, g))
    vt_spec = pl.BlockSpec((None, V_ROWS, s), lambda bi, g, i: (g, 0, bi))

    def q_spec(t):
        return pl.BlockSpec((None, LANES, t), lambda bi, g, i: (g, 0, bi * (s // t) + i))

    def o_spec(t):
        return pl.BlockSpec((None, t, LANES), lambda bi, g, i: (bi, i, g))

    params = pltpu.CompilerParams(vmem_limit_bytes=VMEM_LIMIT,
                                  dimension_semantics=("arbitrary", "arbitrary", "arbitrary"))
    out_shape = jax.ShapeDtypeStruct((b, s, d), F32)
    if kind == KIND_DIFF:
        t = min(512, s)
        lam_init = 0.8 - 0.6 * math.exp(-0.3 * layer)
        vec = pl.BlockSpec((1, HEAD_DIM), lambda bi, g, i: (0, 0))
        return pl.pallas_call(
            functools.partial(_diff_prompt_kernel, t=t, seq=s, lam_init=lam_init),
            grid=(b, groups, s // t),
            in_specs=[smem, q_spec(t), k_spec, vt_spec, vec, vec, vec, vec,
                      pl.BlockSpec((1, LANES), lambda bi, g, i: (0, 0))],
            out_specs=o_spec(t),
            out_shape=out_shape,
            scratch_shapes=[pltpu.VMEM((t, LANES), BF16), pltpu.VMEM((BF16_ROWS, LANES), BF16),
                            pltpu.VMEM((2, V_ROWS, t), F32), pltpu.VMEM((2, 2, t, t), F32)],
            compiler_params=params,
            name="diff_prompt",
        )(_slope_table(DIFF_HEADS), qt, kb, vt, *lam_vecs, sub_gain)
    if kind == KIND_SB:
        tq, tk = min(256, s), 128
        return pl.pallas_call(
            functools.partial(_sb_prompt_kernel, tq=tq, tk=tk),
            grid=(b, groups, s // tq),
            in_specs=[q_spec(tq), k_spec, vt_spec],
            out_specs=o_spec(tq),
            out_shape=out_shape,
            scratch_shapes=[pltpu.VMEM((2, LANES, tq), F32)],
            compiler_params=params,
            name="sb_prompt",
        )(qt, kb, vt)
    tq, tk = min(2 * MOBA_BLOCK, s), MOBA_BLOCK
    nb = s // tk
    means = _block_means(k, layer)
    return pl.pallas_call(
        functools.partial(_moba_prompt_kernel, tq=tq, seq=s),
        grid=(b, groups, s // tq),
        in_specs=[smem, q_spec(tq), k_spec, vt_spec,
                  pl.BlockSpec((None, nb, LANES), lambda bi, g, i: (bi, 0, g))],
        out_specs=o_spec(tq),
        out_shape=out_shape,
        scratch_shapes=[pltpu.VMEM((tk, LANES), BF16), pltpu.VMEM((BF16_ROWS, LANES), BF16),
                        pltpu.VMEM((2, V_ROWS, tq), F32), pltpu.VMEM((2, nb, tq), F32),
                        pltpu.VMEM((2, 2, tk, tq), F32)],
        compiler_params=params,
        name="moba_prompt",
    )(_slope_table(MOBA_HEADS), qt, kb, vt, means)


REPL = 16


def _row_segment(r, kind):
    rep = r % REPL
    if kind == KIND_DIFF:
        return 2 * (rep % DIFF_HEADS) + rep // DIFF_HEADS
    return rep


def _suffix_matrix(tk):
    j = lax.broadcasted_iota(jnp.int32, (2 * tk, tk), 0)
    s = lax.broadcasted_iota(jnp.int32, (2 * tk, tk), 1)
    return jnp.where((j & (tk - 1)) > s, 1.0, 0.0).astype(BF16)


def _sample_kernel(pt_ref, qrep_ref, knew_ref, vnew_ref, slope_ref, lq1, lk1, lq2, lk2, gain_ref,
                   *rest, kind, pages, nkc, past, n_new, lam_init):
    k_refs = rest[:pages]
    v_refs = rest[pages:2 * pages]
    o_ref = rest[2 * pages]
    wq_sc, wqf_sc, s_sc, p_sc, acc_sc, gate_sc = rest[2 * pages + 1:]
    del pt_ref
    step = pl.program_id(1)
    page = k_refs[0].shape[0]
    rows, d = wqf_sc.shape
    width = s_sc.shape[1]
    out_rows = rows // 2 if kind == KIND_DIFF else rows
    row_id = lax.broadcasted_iota(jnp.int32, (rows, 1), 0)
    qpos = past + row_id // REPL

    @pl.when(step == 0)
    def _():
        col = lax.broadcasted_iota(jnp.int32, (1, d), 1)
        wq = jnp.where(col // HEAD_DIM == _row_segment(row_id, kind), qrep_ref[...], 0.0)
        wqf_sc[...] = wq
        wq_sc[...] = (wq * QK_SCALE).astype(BF16)
        gate_sc[...] = jnp.zeros(gate_sc.shape, F32)

    @pl.when(step < nkc)
    def _():
        for pp in range(0, pages, 2):
            k0 = k_refs[pp][...]
            k1 = k_refs[pp + 1][...]
            kk = jnp.concatenate([k0.astype(BF16), k1.astype(BF16)], axis=0)
            col0 = pl.multiple_of((step * pages + pp) * page, 2 * page)
            s_sc[:, pl.ds(col0, 2 * page)] = _dot_nt(wq_sc[...], kk)
            if kind == KIND_MOBA:
                mean = (jnp.sum(k0, axis=0, keepdims=True) + jnp.sum(k1, axis=0, keepdims=True)) * (1.0 / MOBA_BLOCK)
                gcol = jnp.sum(wqf_sc[...] * mean, axis=-1, keepdims=True)
                lane = lax.broadcasted_iota(jnp.int32, gate_sc.shape, 1)
                gate_sc[...] = jnp.where(lane == (step * pages + pp) // 2, gcol, gate_sc[...])

    @pl.when(step == nkc)
    def _():
        s_sc[:, past:past + page] = _dot_nt(wq_sc[...], knew_ref[...])
        col = lax.broadcasted_iota(jnp.int32, (1, width), 1)
        if kind == KIND_SB:
            t2 = _suffix_matrix(page)
            n_chunks = width // page

            def body(it, c_prev):
                c0 = pl.multiple_of((n_chunks - 1 - it) * page, page)
                z = s_sc[:, pl.ds(c0, page)]
                colpos = c0 + lax.broadcasted_iota(jnp.int32, (1, page), 1)
                prior = colpos < qpos
                lk = jnp.where(prior, -_softplus(z), 0.0)
                hi, lo = _split_hi_lo(lk)
                later = c_prev + jnp.dot(jnp.concatenate([hi, lo], axis=1), t2, preferred_element_type=F32)
                p_sc[:, pl.ds(c0, page)] = jnp.where(prior, jnp.exp(z + lk + later), 0.0)
                return c_prev + jnp.sum(lk, axis=-1, keepdims=True)

            lax.fori_loop(0, n_chunks, body, jnp.zeros((rows, 1), F32))
        else:
            dist = (qpos - col).astype(F32)
            ok = dist >= 0.0
            if kind == KIND_MOBA:
                own = past // MOBA_BLOCK
                lane = lax.broadcasted_iota(jnp.int32, gate_sc.shape, 1)
                sel = _top_select(gate_sc[...], lane < own, axis=1)
                sel = jnp.where(lane == own, 1.0, sel)
                blk_row = lax.broadcasted_iota(jnp.int32, (gate_sc.shape[1], 1), 0)
                expand = jnp.where(col // MOBA_BLOCK == blk_row, 1.0, 0.0).astype(BF16)
                picked = jnp.dot(sel.astype(BF16), expand, preferred_element_type=F32)
                ok = jnp.logical_and(ok, picked > 0.5)
            sb = jnp.where(ok, s_sc[...] - slope_ref[...] * dist, NEG_INF)
            e = jnp.exp(sb - jnp.max(sb, axis=-1, keepdims=True))
            p = e / jnp.sum(e, axis=-1, keepdims=True)
            if kind == KIND_DIFF:
                lam = _diff_lambda(lq1, lk1, lq2, lk2, lam_init)
                half = REPL // 2
                for qi in range(n_new):
                    p_sc[qi * half:(qi + 1) * half, :] = (
                        p[qi * REPL:qi * REPL + half] - lam * p[qi * REPL + half:(qi + 1) * REPL])
            else:
                p_sc[...] = p
        acc_sc[0:out_rows, :] = jnp.dot(p_sc[0:out_rows, past:past + page].astype(BF16), vnew_ref[...],
                                        preferred_element_type=F32)

    @pl.when(step >= nkc)
    def _():
        for pp in range(0, pages, 2):
            vv = jnp.concatenate([v_refs[pp][...].astype(BF16), v_refs[pp + 1][...].astype(BF16)], axis=0)
            col0 = pl.multiple_of(((step - nkc) * pages + pp) * page, 2 * page)
            w = p_sc[0:out_rows, pl.ds(col0, 2 * page)].astype(BF16)
            acc_sc[0:out_rows, :] = acc_sc[0:out_rows, :] + jnp.dot(w, vv, preferred_element_type=F32)

    @pl.when(step == 2 * nkc - 1)
    def _():
        per_q = out_rows // n_new
        seg_w = LANES if kind == KIND_DIFF else HEAD_DIM
        r = lax.broadcasted_iota(jnp.int32, (per_q, 1), 0)
        col = lax.broadcasted_iota(jnp.int32, (1, d), 1)
        own_cols = col // seg_w == r
        for qi in range(n_new):
            a = jnp.where(own_cols, acc_sc[qi * per_q:(qi + 1) * per_q, :], 0.0)
            if kind == KIND_DIFF:
                ms = jnp.sum(a * a, axis=-1, keepdims=True) * (1.0 / LANES)
                a = a * lax.rsqrt(ms + LN_EPS) * gain_ref[...] * (1.0 - lam_init)
            o_ref[qi:qi + 1, :] = jnp.sum(a, axis=0, keepdims=True)


def _sample_mixer(kind, layer, page_table, q, k_new, v_new, cache_k, cache_v, lam_vecs, sub_gain):
    db, n_new, d = q.shape
    n_pages = page_table.shape[1]
    page = cache_k.shape[2]
    past = n_pages * page
    pages = 8 if n_pages % 8 == 0 else 2
    nkc = n_pages // pages
    rows = n_new * REPL
    width = past + page
    assert MOBA_BLOCK == 2 * page and past % MOBA_BLOCK == 0 and past // MOBA_BLOCK < LANES
    assert n_pages % pages == 0 and n_new <= page and rows % 16 == 0
    lam_init = 0.8 - 0.6 * math.exp(-0.3 * layer)

    qrep = jnp.repeat(q, REPL, axis=1)
    pad = ((0, 0), (0, page - n_new), (0, 0))
    knew = jnp.pad(k_new, pad).astype(BF16)
    vnew = jnp.pad(v_new, pad).astype(BF16)
    rep = np.arange(rows) % REPL
    if kind == KIND_DIFF:
        slopes = _alibi_slopes(DIFF_HEADS)[rep % DIFF_HEADS]
    elif kind == KIND_MOBA:
        slopes = _alibi_slopes(MOBA_HEADS)[rep]
    else:
        slopes = np.zeros((rows,), np.float32)
    slope_rows = jnp.asarray(slopes.reshape(rows, 1))
    gain_full = jnp.tile(sub_gain, (1, d // LANES))

    def per_sample(shape):
        return pl.BlockSpec((None,) + shape, lambda b, s, pt: (b, 0, 0))

    def const(shape):
        return pl.BlockSpec(shape, lambda b, s, pt: (0, 0))

    def k_spec(p):
        return pl.BlockSpec(
            (None, None, page, d),
            lambda b, s, pt: (layer, pt[b * n_pages + jnp.minimum(s, nkc - 1) * pages + p], 0, 0))

    def v_spec(p):
        return pl.BlockSpec(
            (None, None, page, d),
            lambda b, s, pt: (layer, pt[b * n_pages + jnp.maximum(s - nkc, 0) * pages + p], 0, 0))

    vec = const((1, HEAD_DIM))
    grid_spec = pltpu.PrefetchScalarGridSpec(
        num_scalar_prefetch=1,
        grid=(db, 2 * nkc),
        in_specs=[per_sample((rows, d)), per_sample((page, d)), per_sample((page, d)),
                  const((rows, 1)), vec, vec, vec, vec, const((1, d))]
                 + [k_spec(p) for p in range(pages)] + [v_spec(p) for p in range(pages)],
        out_specs=per_sample((n_new, d)),
        scratch_shapes=[pltpu.VMEM((rows, d), BF16), pltpu.VMEM((rows, d), F32),
                        pltpu.VMEM((rows, width), F32), pltpu.VMEM((rows, width), F32),
                        pltpu.VMEM((rows, d), F32), pltpu.VMEM((rows, LANES), F32)],
    )
    return pl.pallas_call(
        functools.partial(_sample_kernel, kind=kind, pages=pages, nkc=nkc, past=past,
                          n_new=n_new, lam_init=lam_init),
        grid_spec=grid_spec,
        out_shape=jax.ShapeDtypeStruct((db, n_new, d), F32),
        compiler_params=pltpu.CompilerParams(vmem_limit_bytes=VMEM_LIMIT),
        name=("diff_sample", "sb_sample", "moba_sample")[kind],
    )(page_table.reshape(-1), qrep, knew, vnew, slope_rows, *lam_vecs, gain_full,
      *([cache_k] * pages), *([cache_v] * pages))


def kernel(x_prompt, x_sample, cache_k, cache_v, page_table, w_in, w_out, ln_gain, ln_bias,
           diff_lambda_q1, diff_lambda_k1, diff_lambda_q2, diff_lambda_k2, diff_subln_gain):
    depth = w_in.shape[0]
    b, s, d = x_prompt.shape
    db, ds, _ = x_sample.shape
    alpha = (2 * depth) ** 0.25
    tm_in = min(256, b * s)
    tm_out = min(512, b * s)
    xp = x_prompt.reshape(b * s, d)
    xs = x_sample.reshape(db * ds, d)
    k_all = v_all = None
    ks_rows, vs_rows = [], []
    for i in range(depth):
        kind = i % N_MIXERS
        j = i // N_MIXERS
        lam_vecs = [v[j].reshape(1, HEAD_DIM) for v in
                    (diff_lambda_q1, diff_lambda_k1, diff_lambda_q2, diff_lambda_k2)]
        sub_gain = diff_subln_gain[j].reshape(1, LANES)
        w_in_b = w_in[i].astype(BF16)
        w_out_b = w_out[i].astype(BF16)

        qt, k_all, kb, v_all, vt, g = _in_proj_prompt(xp, w_in_b[:, 0:d].T, w_in_b[:, 2 * d:3 * d].T,
                                                      w_in_b[:, d:4 * d], tm_in, i, depth, k_all, v_all)
        o = _prompt_mixer(kind, i, b, s, qt, k_all.reshape(depth, b, s, d), kb.reshape(b, s, d), vt,
                          lam_vecs, sub_gain)
        xp = _out_proj(o.reshape(b * s, d), g, xp, w_out_b, ln_gain[i], ln_bias[i], alpha, tm_out)

        qs, ks, vs, gs = _in_proj_rows(xs, w_in_b, db * ds)
        sshp = (db, ds, d)
        os_ = _sample_mixer(kind, i, page_table, qs.reshape(sshp), ks.reshape(sshp), vs.reshape(sshp),
                            cache_k, cache_v, lam_vecs, sub_gain)
        xs = _out_proj(os_.reshape(db * ds, d), gs, xs, w_out_b, ln_gain[i], ln_bias[i], alpha, db * ds)
        ks_rows.append(ks.reshape(sshp))
        vs_rows.append(vs.reshape(sshp))
    return (xp.reshape(b, s, d), xs.reshape(db, ds, d), k_all.reshape(depth, b, s, d),
            v_all.reshape(depth, b, s, d), jnp.stack(ks_rows), jnp.stack(vs_rows))
```

```python
import functools
import math

import ml_dtypes
import numpy as np
import jax
import jax.numpy as jnp
from jax import lax
from jax.experimental import pallas as pl
from jax.experimental.pallas import tpu as pltpu

F32 = jnp.float32
BF16 = jnp.bfloat16

N_MIXERS = 3
DIFF_HEADS = 8
HEAD_DIM = 64
SB_HEADS = 16
MOBA_HEADS = 16
MOBA_BLOCK = 256
MOBA_TOPK = 3
LN_EPS = 1e-5
QK_SCALE = HEAD_DIM ** -0.5
LOG2E = 1.4426950408889634
LN2 = 0.6931471805599453

LANES = 128
BF16_ROWS = 16
V_ROWS = LANES + BF16_ROWS
VMEM_LIMIT = 56 * 1024 * 1024

SB_LOG_ZERO = -104.5
ALIBI_LOG_ZERO = -106.0

KIND_DIFF, KIND_SB, KIND_MOBA = 0, 1, 2
NEG_INF = float("-inf")


def _alibi_slopes(n_heads):
    return np.asarray(2.0 ** (-8.0 * np.arange(1, n_heads + 1) / n_heads), dtype=np.float32)


def _slope_table(n_heads):
    s = _alibi_slopes(n_heads)
    rest = (s.astype(np.float64) * LOG2E).astype(np.float32)
    parts = []
    for _ in range(3):
        p = rest.astype(ml_dtypes.bfloat16).astype(np.float32)
        parts.append(p)
        rest = rest - p
    return jnp.asarray(np.stack([s, (1.0 / s).astype(np.float32)] + parts))


def _alibi_key_codes(tk):
    r = lax.broadcasted_iota(jnp.int32, (tk, LANES), 0)
    lane = lax.broadcasted_iota(jnp.int32, (tk, LANES), 1)
    code = jnp.where(lane < 3, r - (r & 1), jnp.where(lane < 6, r & 1, 0))
    return code.astype(F32).astype(BF16)


def _alibi_query_rows(parts, tq):
    row = lax.broadcasted_iota(jnp.int32, (LANES, tq), 0)
    val = jnp.where(row % 3 == 0, parts[0], jnp.where(row % 3 == 1, parts[1], parts[2]))
    return jnp.where(row < 6, val, 0.0).astype(BF16)


def _dot_nt(a, b):
    return lax.dot_general(a, b, (((1,), (1,)), ((), ())), preferred_element_type=F32)


def _diff_lambda(lq1, lk1, lq2, lk2, lam_init):
    a = jnp.exp(jnp.sum(lq1[...] * lk1[...], axis=-1, keepdims=True))
    b = jnp.exp(jnp.sum(lq2[...] * lk2[...], axis=-1, keepdims=True))
    return a - b + lam_init


def _top_select(gate, valid, axis):
    pos = lax.broadcasted_iota(jnp.int32, gate.shape, axis).astype(F32)
    g = jnp.where(valid, gate, NEG_INF)
    sel = jnp.zeros(gate.shape, F32)
    for _ in range(MOBA_TOPK):
        m = jnp.max(g, axis=axis, keepdims=True)
        idx = jnp.min(jnp.where(g == m, pos, 1e9), axis=axis, keepdims=True)
        idx = jnp.where(m > NEG_INF, idx, 1e9)
        pick = pos == idx
        sel = jnp.where(pick, 1.0, sel)
        g = jnp.where(pick, NEG_INF, g)
    return sel


def _softplus(z):
    return jnp.maximum(z, 0.0) + jnp.log1p(jnp.exp(-jnp.abs(z)))


def _split_hi_lo(x):
    hi = x.astype(BF16)
    return hi, (x - hi.astype(F32)).astype(BF16)


def _in_proj_rows_kernel(x_ref, w_ref, q_ref, k_ref, v_ref, g_ref):
    xb = x_ref[...].astype(BF16)
    d = q_ref.shape[-1]
    for j, o_ref in enumerate((q_ref, k_ref, v_ref, g_ref)):
        o_ref[...] = jnp.dot(xb, w_ref[:, j * d:(j + 1) * d], preferred_element_type=F32)


def _in_proj_rows(x2d, w_bf16, tm):
    n, d = x2d.shape
    row = pl.BlockSpec((tm, d), lambda i: (i, 0))
    return pl.pallas_call(
        _in_proj_rows_kernel,
        grid=(n // tm,),
        in_specs=[row, pl.BlockSpec((d, 4 * d), lambda i: (0, 0))],
        out_specs=[row] * 4,
        out_shape=[jax.ShapeDtypeStruct((n, d), F32)] * 4,
        compiler_params=pltpu.CompilerParams(vmem_limit_bytes=VMEM_LIMIT),
        name="in_proj_rows",
    )(x2d, w_bf16)


def _in_proj_prompt_kernel(x_ref, wqt_ref, wvt_ref, w_ref, *rest):
    qt_ref, k_ref, kb_ref, v_ref, vt_ref, g_ref = rest[-6:]
    xb = x_ref[...].astype(BF16)
    d = k_ref.shape[-1]
    qt = _dot_nt(wqt_ref[...], xb)
    vt = _dot_nt(wvt_ref[...], xb).astype(BF16)
    ones = jnp.ones((V_ROWS - LANES, vt.shape[1]), BF16)
    for gi in range(d // LANES):
        qt_ref[gi] = qt[gi * LANES:(gi + 1) * LANES, :]
        vt_ref[gi, 0:LANES, :] = vt[gi * LANES:(gi + 1) * LANES, :]
        vt_ref[gi, LANES:V_ROWS, :] = ones
    k = jnp.dot(xb, w_ref[:, 0:d], preferred_element_type=F32)
    k_ref[...] = k
    kb_ref[...] = k.astype(BF16)
    v_ref[...] = jnp.dot(xb, w_ref[:, d:2 * d], preferred_element_type=F32)
    g_ref[...] = jnp.dot(xb, w_ref[:, 2 * d:3 * d], preferred_element_type=F32)


def _in_proj_prompt(x2d, wqt, wvt, w_kvg, tm, layer, depth, k_all, v_all):
    n, d = x2d.shape
    groups = d // LANES
    row = pl.BlockSpec((tm, d), lambda i: (i, 0))
    sq = pl.BlockSpec((d, d), lambda i: (0, 0))
    layer_row = pl.BlockSpec((None, tm, d), lambda i: (layer, i, 0))
    stack = jax.ShapeDtypeStruct((depth, n, d), F32)
    in_specs = [row, sq, sq, pl.BlockSpec((d, 3 * d), lambda i: (0, 0))]
    args = [x2d, wqt, wvt, w_kvg]
    aliases = {}
    if k_all is not None:
        in_specs += [pl.BlockSpec(memory_space=pl.ANY)] * 2
        args += [k_all, v_all]
        aliases = {4: 1, 5: 3}
    return pl.pallas_call(
        _in_proj_prompt_kernel,
        grid=(n // tm,),
        in_specs=in_specs,
        out_specs=[pl.BlockSpec((groups, LANES, tm), lambda i: (0, 0, i)), layer_row, row, layer_row,
                   pl.BlockSpec((groups, V_ROWS, tm), lambda i: (0, 0, i)), row],
        out_shape=[jax.ShapeDtypeStruct((groups, LANES, n), F32), stack, jax.ShapeDtypeStruct((n, d), BF16),
                   stack, jax.ShapeDtypeStruct((groups, V_ROWS, n), BF16), jax.ShapeDtypeStruct((n, d), F32)],
        input_output_aliases=aliases,
        compiler_params=pltpu.CompilerParams(vmem_limit_bytes=VMEM_LIMIT),
        name="in_proj_prompt",
    )(*args)


def _out_proj_kernel(o_ref, g_ref, x_ref, w_ref, gain_ref, bias_ref, y_ref, *, alpha):
    gt = g_ref[...]
    og = o_ref[...] * (gt / (1.0 + jnp.exp(-gt)))
    y = jnp.dot(og.astype(BF16), w_ref[...], preferred_element_type=F32)
    z = alpha * x_ref[...] + y
    mu = jnp.mean(z, axis=-1, keepdims=True)
    zc = z - mu
    var = jnp.mean(zc * zc, axis=-1, keepdims=True)
    y_ref[...] = zc * lax.rsqrt(var + LN_EPS) * gain_ref[...] + bias_ref[...]


def _out_proj(o2d, g2d, x2d, w_bf16, gain, bias, alpha, tm):
    n, d = x2d.shape
    row = pl.BlockSpec((tm, d), lambda i: (i, 0))
    vec = pl.BlockSpec((1, d), lambda i: (0, 0))
    return pl.pallas_call(
        functools.partial(_out_proj_kernel, alpha=alpha),
        grid=(n // tm,),
        in_specs=[row, row, row, pl.BlockSpec((d, d), lambda i: (0, 0)), vec, vec],
        out_specs=row,
        out_shape=jax.ShapeDtypeStruct((n, d), F32),
        compiler_params=pltpu.CompilerParams(vmem_limit_bytes=VMEM_LIMIT),
        name="out_proj_ln",
    )(o2d, g2d, x2d, w_bf16, gain.reshape(1, d), bias.reshape(1, d))


def _split_maps_t(qt):
    row = lax.broadcasted_iota(jnp.int32, qt.shape, 0)
    return [jnp.where(row < HEAD_DIM, qt, 0.0), jnp.where(row >= HEAD_DIM, qt, 0.0)]


def _key_abs_max(k_ref, seq, chunk):
    def body(it, acc):
        kc = k_ref[pl.ds(pl.multiple_of(it * chunk, chunk), chunk), :].astype(F32)
        return jnp.maximum(acc, jnp.max(jnp.abs(kc), axis=0, keepdims=True))

    kmax = lax.fori_loop(0, seq // chunk, body, jnp.zeros((1, LANES), F32))
    return jnp.broadcast_to(kmax, (BF16_ROWS, LANES)).astype(BF16)


def _score_bound(kabs, qm_f32):
    return jnp.dot(kabs, jnp.abs(qm_f32).astype(BF16), preferred_element_type=F32)[0:1]


def _first_live_tile(excess, slope_inv, q0, tk, hi):
    e = jnp.max(excess, axis=-1, keepdims=True) * LN2
    lim = ((ALIBI_LOG_ZERO - e) * slope_inv + (q0 - tk + 1).astype(F32)) * (1.0 / tk)
    j_lo = jnp.clip(jnp.ceil(lim), 0.0, hi.astype(F32)).astype(jnp.int32)
    return jnp.max(j_lo)


def _softmax_tile(u_ref, m_tile, shift, m_prev, vt, acc_ref):
    m_new = jnp.maximum(m_prev, m_tile + shift)
    m_fin = jnp.where(m_new > NEG_INF, m_new, 0.0)
    p = jnp.exp2(u_ref[...] - (m_fin - shift)).astype(BF16)
    pv = jnp.dot(vt, p, preferred_element_type=F32)
    acc_ref[...] = jnp.exp2(m_prev - m_fin) * acc_ref[...] + pv
    return m_new


def _pipelined_sweep(n_diag, first_diag, scores, update, first_live, m_init):
    m = m_init
    mt_cur = scores(first_diag, 0, 0)
    j_lo = None
    last_past = jnp.maximum(first_diag - 1, 0)
    for n in range(n_diag):
        if n + 1 < n_diag:
            mt_nxt = scores(first_diag + n + 1, (n + 1) & 1, n + 1)
        else:
            j_lo = first_live(m, mt_cur, n)
            mt_nxt = scores(jnp.minimum(j_lo, last_past), n_diag & 1, None)
        m = update(first_diag + n, n & 1, mt_cur, m, True)
        mt_cur = mt_nxt

    slot_a = n_diag & 1
    slot_b = slot_a ^ 1

    def body(pair, carry):
        m_prev, mt_a = carry
        a = j_lo + 2 * pair
        b = jnp.minimum(a + 1, last_past)
        mt_b = scores(b, slot_b, None)
        m_mid = update(a, slot_a, mt_a, m_prev, True)
        mt_next = scores(jnp.minimum(a + 2, last_past), slot_a, None)
        return update(b, slot_b, mt_b, m_mid, a + 1 < first_diag), mt_next

    lax.fori_loop(0, (first_diag - j_lo + 1) // 2, body, (m, mt_cur))


def _diff_prompt_kernel(slope_ref, qt_ref, k_ref, vt_ref, lq1, lk1, lq2, lk2, gain_ref, o_ref,
                        kcode_sc, kabs_sc, acc_sc, u_sc, *, t, seq, lam_init):
    g = pl.program_id(1)
    i = pl.program_id(2)
    slope2 = slope_ref[0, g] * LOG2E

    @pl.when(i == 0)
    def _():
        kcode_sc[...] = _alibi_key_codes(t)
        kabs_sc[...] = _key_abs_max(k_ref, seq, t)

    qf = _split_maps_t(qt_ref[...] * (QK_SCALE * LOG2E))
    slope_rows = _alibi_query_rows([slope_ref[2 + e, g] for e in range(3)], t)
    qm = [jnp.concatenate([x.astype(BF16), slope_rows], axis=0) for x in qf]
    q0 = i * t
    acc_sc[...] = jnp.zeros(acc_sc.shape, F32)

    def scores(j, slot, diag_n):
        kt = jnp.concatenate([k_ref[pl.ds(pl.multiple_of(j * t, t), t), :], kcode_sc[...]], axis=1)
        maxima = []
        for c in range(2):
            u = jnp.dot(kt, qm[c], preferred_element_type=F32)
            if diag_n is not None:
                krow = lax.broadcasted_iota(jnp.int32, (t, t), 0)
                qcol = lax.broadcasted_iota(jnp.int32, (t, t), 1)
                u = jnp.where(krow <= qcol, u, NEG_INF)
            u_sc[slot, c] = u
            maxima.append(jnp.max(u, axis=0, keepdims=True))
        return tuple(maxima)

    def update(j, slot, maxima, m_prev, valid):
        k0 = pl.multiple_of(j * t, t)
        vt = vt_ref[:, pl.ds(k0, t)]
        shift = slope2 * (jnp.zeros((1, t), jnp.int32) + (k0 - q0)).astype(F32)
        if valid is not True:
            shift = jnp.where(valid, shift, NEG_INF)
        return tuple(_softmax_tile(u_sc.at[slot, c], maxima[c], shift, m_prev[c], vt, acc_sc.at[c])
                     for c in range(2))

    def first_live(m_prev, maxima, n):
        del m_prev, n
        kabs = kabs_sc[...]
        excess = jnp.maximum(_score_bound(kabs, qf[0]) - maxima[0], _score_bound(kabs, qf[1]) - maxima[1])
        return _first_live_tile(excess, slope_ref[1, g], q0, t, i)

    neg = jnp.full((1, t), NEG_INF, F32)
    _pipelined_sweep(1, i, scores, update, first_live, (neg, neg))

    lam = _diff_lambda(lq1, lk1, lq2, lk2, lam_init)
    o = (acc_sc[0, 0:LANES, :] / acc_sc[0, LANES:LANES + 1, :]
         - lam * (acc_sc[1, 0:LANES, :] / acc_sc[1, LANES:LANES + 1, :]))
    ms = jnp.mean(o * o, axis=0, keepdims=True)
    o_ref[...] = (o * lax.rsqrt(ms + LN_EPS)).T * gain_ref[...] * (1.0 - lam_init)


def _moba_prompt_kernel(slope_ref, qt_ref, k_ref, vt_ref, mean_ref, o_ref,
                        kcode_sc, kabs_sc, acc_sc, sel_sc, u_sc, *, tq, seq):
    tk = MOBA_BLOCK
    n_diag = tq // tk
    g = pl.program_id(1)
    i = pl.program_id(2)
    slopes = [slope_ref[0, 2 * g] * LOG2E, slope_ref[0, 2 * g + 1] * LOG2E]

    @pl.when(i == 0)
    def _():
        kcode_sc[...] = _alibi_key_codes(tk)
        kabs_sc[...] = _key_abs_max(k_ref, seq, tk)

    qf = _split_maps_t(qt_ref[...])
    qs = [x * (QK_SCALE * LOG2E) for x in qf]
    qm = [jnp.concatenate([qs[h].astype(BF16),
                           _alibi_query_rows([slope_ref[2 + e, 2 * g + h] for e in range(3)], tq)], axis=0)
          for h in range(2)]
    means = mean_ref[...]
    first_diag = i * n_diag
    col_blk = lax.broadcasted_iota(jnp.int32, (1, tq), 1) // tk
    blk = lax.broadcasted_iota(jnp.int32, (means.shape[0], tq), 0)
    for h in range(2):
        gate = jnp.dot(means, qf[h], precision=lax.Precision.HIGHEST, preferred_element_type=F32)
        sel_sc[h] = _top_select(gate, blk < first_diag + col_blk, axis=0)
    q0 = i * tq
    acc_sc[...] = jnp.zeros(acc_sc.shape, F32)

    def scores(j, slot, diag_n):
        kt = jnp.concatenate([k_ref[pl.ds(pl.multiple_of(j * tk, tk), tk), :], kcode_sc[...]], axis=1)
        maxima = []
        for h in range(2):
            u = jnp.dot(kt, qm[h], preferred_element_type=F32)
            picked = sel_sc[h, pl.ds(j, 1), :]
            if diag_n is None:
                u = jnp.where(picked > 0.0, u, NEG_INF)
            else:
                allowed = jnp.where(col_blk == diag_n, 1.0, picked)
                krow = lax.broadcasted_iota(jnp.int32, (tk, tq), 0) + diag_n * tk
                qcol = lax.broadcasted_iota(jnp.int32, (tk, tq), 1)
                u = jnp.where(krow <= qcol, jnp.where(allowed > 0.0, u, NEG_INF), NEG_INF)
            u_sc[slot, h] = u
            maxima.append(jnp.max(u, axis=0, keepdims=True))
        return tuple(maxima)

    def update(j, slot, maxima, m_prev, valid):
        k0 = pl.multiple_of(j * tk, tk)
        vt = vt_ref[:, pl.ds(k0, tk)]
        rel = (jnp.zeros((1, tq), jnp.int32) + (k0 - q0)).astype(F32)
        if valid is not True:
            rel = jnp.where(valid, rel, NEG_INF)
        return tuple(_softmax_tile(u_sc.at[slot, h], maxima[h], slopes[h] * rel, m_prev[h], vt, acc_sc.at[h])
                     for h in range(2))

    def first_live(m_prev, maxima, n):
        kabs = kabs_sc[...]
        j_lo = None
        for h in range(2):
            m_h = jnp.maximum(m_prev[h], maxima[h] + slopes[h] * float(n * tk))
            j_h = _first_live_tile(_score_bound(kabs, qs[h]) - m_h, slope_ref[1, 2 * g + h], q0, tk, first_diag)
            j_lo = j_h if j_lo is None else jnp.minimum(j_lo, j_h)
        return j_lo

    neg = jnp.full((1, tq), NEG_INF, F32)
    _pipelined_sweep(n_diag, first_diag, scores, update, first_live, (neg, neg))

    t = tq
    row = lax.broadcasted_iota(jnp.int32, (LANES, t), 0)
    o = jnp.where(row < HEAD_DIM, acc_sc[0, 0:LANES, :] / acc_sc[0, LANES:LANES + 1, :],
                  acc_sc[1, 0:LANES, :] / acc_sc[1, LANES:LANES + 1, :])
    o_ref[...] = o.T


def _suffix_matrix_t(tk):
    s = lax.broadcasted_iota(jnp.int32, (tk + BF16_ROWS, 2 * tk), 0)
    j = lax.broadcasted_iota(jnp.int32, (tk + BF16_ROWS, 2 * tk), 1)
    return jnp.where(s >= tk, 1.0, jnp.where((j & (tk - 1)) > s, 1.0, 0.0)).astype(BF16)


def _sb_prompt_kernel(*refs, batch, seq, tq, tk):
    qt_refs = refs[:batch]
    k_ref, vt_ref, o_ref, acc_sc = refs[batch:]
    i = pl.program_id(1)
    chains = [(b, h) for b in range(batch) for h in range(2)]
    qm = {}
    for b in range(batch):
        for h, x in enumerate(_split_maps_t(qt_refs[b][...] * QK_SCALE)):
            qm[b, h] = x.astype(BF16)
    acc_sc[...] = jnp.zeros(acc_sc.shape, F32)
    qpos = i * tq + lax.broadcasted_iota(jnp.int32, (tk, tq), 1)
    krow = lax.broadcasted_iota(jnp.int32, (tk, tq), 0)
    tmat = _suffix_matrix_t(tk)

    def chunk(j, c_prev):
        k0 = pl.multiple_of(j * tk, tk)
        kt = [k_ref[b, pl.ds(k0, tk), :] for b in range(batch)]
        vt = [vt_ref[0:LANES, pl.ds(pl.multiple_of(b * seq + k0, tk), tk)] for b in range(batch)]
        past = (k0 + krow) < qpos
        z = [jnp.dot(kt[b], qm[b, h], preferred_element_type=F32) for b, h in chains]
        lk = [jnp.where(past, -_softplus(zz), 0.0) for zz in z]
        suf = [jnp.dot(tmat, jnp.concatenate(_split_hi_lo(x), axis=0), preferred_element_type=F32) for x in lk]
        c_new = []
        for n, (b, h) in enumerate(chains):
            later = c_prev[n] + suf[n][0:tk]
            a = jnp.where(past, jnp.exp(z[n] + lk[n] + later), 0.0)
            acc_sc[b, h] = acc_sc[b, h] + jnp.dot(vt[b], a.astype(BF16), preferred_element_type=F32)
            c_new.append(c_prev[n] + suf[n][tk:tk + 1])
        return tuple(c_new)

    def cond(carry):
        return jnp.logical_and(carry[0] >= 0, carry[1] > 0)

    def body(carry):
        c = chunk(carry[0], carry[2:])
        c_max = c[0]
        for x in c[1:]:
            c_max = jnp.maximum(c_max, x)
        live = jnp.max(c_max) > SB_LOG_ZERO
        return (carry[0] - 1, live.astype(jnp.int32)) + c

    zero = jnp.zeros((1, tq), F32)
    lax.while_loop(cond, body, (i * (tq // tk) + tq // tk - 1, jnp.int32(1)) + (zero,) * len(chains))
    row = lax.broadcasted_iota(jnp.int32, (LANES, tq), 0)
    for b in range(batch):
        o_ref[b] = jnp.where(row < HEAD_DIM, acc_sc[b, 0], acc_sc[b, 1]).T


def _block_mean_kernel(k_ref, o_ref, *, per_step):
    k = k_ref[...]
    o_ref[...] = jnp.mean(k.reshape(per_step, MOBA_BLOCK, k.shape[-1]), axis=1)


def _block_means(k_all, layer):
    _, b, s, d = k_all.shape
    nb = s // MOBA_BLOCK
    per_step = 8 if nb % 8 == 0 else nb
    return pl.pallas_call(
        functools.partial(_block_mean_kernel, per_step=per_step),
        grid=(b, nb // per_step),
        in_specs=[pl.BlockSpec((None, None, per_step * MOBA_BLOCK, d), lambda bi, n: (layer, bi, n, 0))],
        out_specs=pl.BlockSpec((None, per_step, d), lambda bi, n: (bi, n, 0)),
        out_shape=jax.ShapeDtypeStruct((b, nb, d), F32),
        compiler_params=pltpu.CompilerParams(vmem_limit_bytes=VMEM_LIMIT),
        name="moba_block_means",
    )(k_all)


def _prompt_mixer(kind, layer, b, s, qt, k, kb, vt, lam_vecs, sub_gain):
    d = k.shape[-1]
    groups = d // LANES
    smem = pl.BlockSpec(memory_space=pltpu.SMEM)
    k_spec = pl.BlockSpec((None, s, LANES), lambda bi, g, i: (bi, 0, g))
    vt_spec = pl.BlockSpec((None, V_ROWS, s), lambda bi, g, i: (g, 0, bi))

    def q_spec(t):
        return pl.BlockSpec((None, LANES, t), lambda bi, g, i: (g, 0, bi * (s // t) + i))

    def o_spec(t):
        return pl.BlockSpec((None, t, LANES), lambda bi, g, i: (bi, i, g))

    params = pltpu.CompilerParams(vmem_limit_bytes=VMEM_LIMIT,
                                  dimension_semantics=("arbitrary", "arbitrary", "arbitrary"))
    out_shape = jax.ShapeDtypeStruct((b, s, d), F32)
    if kind == KIND_DIFF:
        t = min(512, s)
        lam_init = 0.8 - 0.6 * math.exp(-0.3 * layer)
        vec = pl.BlockSpec((1, HEAD_DIM), lambda bi, g, i: (0, 0))
        return pl.pallas_call(
            functools.partial(_diff_prompt_kernel, t=t, seq=s, lam_init=lam_init),
            grid=(b, groups, s // t),
            in_specs=[smem, q_spec(t), k_spec, vt_spec, vec, vec, vec, vec,
                      pl.BlockSpec((1, LANES), lambda bi, g, i: (0, 0))],
            out_specs=o_spec(t),
            out_shape=out_shape,
            scratch_shapes=[pltpu.VMEM((t, LANES), BF16), pltpu.VMEM((BF16_ROWS, LANES), BF16),
                            pltpu.VMEM((2, V_ROWS, t), F32), pltpu.VMEM((2, 2, t, t), F32)],
            compiler_params=params,
            name="diff_prompt",
        )(_slope_table(DIFF_HEADS), qt, kb, vt, *lam_vecs, sub_gain)
    if kind == KIND_SB:
        tq, tk = min(256, s), 128
        nq = s // tq
        return pl.pallas_call(
            functools.partial(_sb_prompt_kernel, batch=b, seq=s, tq=tq, tk=tk),
            grid=(groups, nq),
            in_specs=[pl.BlockSpec((None, LANES, tq), lambda g, i, bb=bb: (g, 0, bb * nq + i)) for bb in range(b)]
                     + [pl.BlockSpec((b, s, LANES), lambda g, i: (0, 0, g)),
                        pl.BlockSpec((None, V_ROWS, b * s), lambda g, i: (g, 0, 0))],
            out_specs=pl.BlockSpec((b, tq, LANES), lambda g, i: (0, i, g)),
            out_shape=out_shape,
            scratch_shapes=[pltpu.VMEM((b, 2, LANES, tq), F32)],
            compiler_params=pltpu.CompilerParams(vmem_limit_bytes=VMEM_LIMIT,
                                                 dimension_semantics=("arbitrary", "arbitrary")),
            name="sb_prompt",
        )(*([qt] * b), kb, vt)
    tq, tk = min(2 * MOBA_BLOCK, s), MOBA_BLOCK
    nb = s // tk
    means = _block_means(k, layer)
    return pl.pallas_call(
        functools.partial(_moba_prompt_kernel, tq=tq, seq=s),
        grid=(b, groups, s // tq),
        in_specs=[smem, q_spec(tq), k_spec, vt_spec,
                  pl.BlockSpec((None, nb, LANES), lambda bi, g, i: (bi, 0, g))],
        out_specs=o_spec(tq),
        out_shape=out_shape,
        scratch_shapes=[pltpu.VMEM((tk, LANES), BF16), pltpu.VMEM((BF16_ROWS, LANES), BF16),
                        pltpu.VMEM((2, V_ROWS, tq), F32), pltpu.VMEM((2, nb, tq), F32),
                        pltpu.VMEM((2, 2, tk, tq), F32)],
        compiler_params=params,
        name="moba_prompt",
    )(_slope_table(MOBA_HEADS), qt, kb, vt, means)


REPL = 16


def _row_segment(r, kind):
    rep = r % REPL
    if kind == KIND_DIFF:
        return 2 * (rep % DIFF_HEADS) + rep // DIFF_HEADS
    return rep


def _suffix_matrix(tk):
    j = lax.broadcasted_iota(jnp.int32, (2 * tk, tk), 0)
    s = lax.broadcasted_iota(jnp.int32, (2 * tk, tk), 1)
    return jnp.where((j & (tk - 1)) > s, 1.0, 0.0).astype(BF16)


def _sample_kernel(pt_ref, qrep_ref, knew_ref, vnew_ref, slope_ref, lq1, lk1, lq2, lk2, gain_ref,
                   *rest, kind, pages, nkc, past, n_new, lam_init):
    k_refs = rest[:pages]
    v_refs = rest[pages:2 * pages]
    o_ref = rest[2 * pages]
    wq_sc, wqf_sc, s_sc, p_sc, acc_sc, gate_sc = rest[2 * pages + 1:]
    del pt_ref
    step = pl.program_id(1)
    page = k_refs[0].shape[0]
    rows, d = wqf_sc.shape
    width = s_sc.shape[1]
    out_rows = rows // 2 if kind == KIND_DIFF else rows
    row_id = lax.broadcasted_iota(jnp.int32, (rows, 1), 0)
    qpos = past + row_id // REPL

    @pl.when(step == 0)
    def _():
        col = lax.broadcasted_iota(jnp.int32, (1, d), 1)
        wq = jnp.where(col // HEAD_DIM == _row_segment(row_id, kind), qrep_ref[...], 0.0)
        wqf_sc[...] = wq
        wq_sc[...] = (wq * QK_SCALE).astype(BF16)
        gate_sc[...] = jnp.zeros(gate_sc.shape, F32)

    @pl.when(step < nkc)
    def _():
        for pp in range(0, pages, 2):
            k0 = k_refs[pp][...]
            k1 = k_refs[pp + 1][...]
            kk = jnp.concatenate([k0.astype(BF16), k1.astype(BF16)], axis=0)
            col0 = pl.multiple_of((step * pages + pp) * page, 2 * page)
            s_sc[:, pl.ds(col0, 2 * page)] = _dot_nt(wq_sc[...], kk)
            if kind == KIND_MOBA:
                mean = (jnp.sum(k0, axis=0, keepdims=True) + jnp.sum(k1, axis=0, keepdims=True)) * (1.0 / MOBA_BLOCK)
                gcol = jnp.sum(wqf_sc[...] * mean, axis=-1, keepdims=True)
                lane = lax.broadcasted_iota(jnp.int32, gate_sc.shape, 1)
                gate_sc[...] = jnp.where(lane == (step * pages + pp) // 2, gcol, gate_sc[...])

    @pl.when(step == nkc)
    def _():
        s_sc[:, past:past + page] = _dot_nt(wq_sc[...], knew_ref[...])
        col = lax.broadcasted_iota(jnp.int32, (1, width), 1)
        if kind == KIND_SB:
            t2 = _suffix_matrix(page)
            n_chunks = width // page

            def body(it, c_prev):
                c0 = pl.multiple_of((n_chunks - 1 - it) * page, page)
                z = s_sc[:, pl.ds(c0, page)]
                colpos = c0 + lax.broadcasted_iota(jnp.int32, (1, page), 1)
                prior = colpos < qpos
                lk = jnp.where(prior, -_softplus(z), 0.0)
                hi, lo = _split_hi_lo(lk)
                later = c_prev + jnp.dot(jnp.concatenate([hi, lo], axis=1), t2, preferred_element_type=F32)
                p_sc[:, pl.ds(c0, page)] = jnp.where(prior, jnp.exp(z + lk + later), 0.0)
                return c_prev + jnp.sum(lk, axis=-1, keepdims=True)

            lax.fori_loop(0, n_chunks, body, jnp.zeros((rows, 1), F32))
        else:
            dist = (qpos - col).astype(F32)
            ok = dist >= 0.0
            if kind == KIND_MOBA:
                own = past // MOBA_BLOCK
                lane = lax.broadcasted_iota(jnp.int32, gate_sc.shape, 1)
                sel = _top_select(gate_sc[...], lane < own, axis=1)
                sel = jnp.where(lane == own, 1.0, sel)
                blk_row = lax.broadcasted_iota(jnp.int32, (gate_sc.shape[1], 1), 0)
                expand = jnp.where(col // MOBA_BLOCK == blk_row, 1.0, 0.0).astype(BF16)
                picked = jnp.dot(sel.astype(BF16), expand, preferred_element_type=F32)
                ok = jnp.logical_and(ok, picked > 0.5)
            sb = jnp.where(ok, s_sc[...] - slope_ref[...] * dist, NEG_INF)
            e = jnp.exp(sb - jnp.max(sb, axis=-1, keepdims=True))
            p = e / jnp.sum(e, axis=-1, keepdims=True)
            if kind == KIND_DIFF:
                lam = _diff_lambda(lq1, lk1, lq2, lk2, lam_init)
                half = REPL // 2
                for qi in range(n_new):
                    p_sc[qi * half:(qi + 1) * half, :] = (
                        p[qi * REPL:qi * REPL + half] - lam * p[qi * REPL + half:(qi + 1) * REPL])
            else:
                p_sc[...] = p
        acc_sc[0:out_rows, :] = jnp.dot(p_sc[0:out_rows, past:past + page].astype(BF16), vnew_ref[...],
                                        preferred_element_type=F32)

    @pl.when(step >= nkc)
    def _():
        for pp in range(0, pages, 2):
            vv = jnp.concatenate([v_refs[pp][...].astype(BF16), v_refs[pp + 1][...].astype(BF16)], axis=0)
            col0 = pl.multiple_of(((step - nkc) * pages + pp) * page, 2 * page)
            w = p_sc[0:out_rows, pl.ds(col0, 2 * page)].astype(BF16)
            acc_sc[0:out_rows, :] = acc_sc[0:out_rows, :] + jnp.dot(w, vv, preferred_element_type=F32)

    @pl.when(step == 2 * nkc - 1)
    def _():
        per_q = out_rows // n_new
        seg_w = LANES if kind == KIND_DIFF else HEAD_DIM
        r = lax.broadcasted_iota(jnp.int32, (per_q, 1), 0)
        col = lax.broadcasted_iota(jnp.int32, (1, d), 1)
        own_cols = col // seg_w == r
        for qi in range(n_new):
            a = jnp.where(own_cols, acc_sc[qi * per_q:(qi + 1) * per_q, :], 0.0)
            if kind == KIND_DIFF:
                ms = jnp.sum(a * a, axis=-1, keepdims=True) * (1.0 / LANES)
                a = a * lax.rsqrt(ms + LN_EPS) * gain_ref[...] * (1.0 - lam_init)
            o_ref[qi:qi + 1, :] = jnp.sum(a, axis=0, keepdims=True)


def _sample_stream_kernel(pt_ref, qrep_ref, knew_ref, vnew_ref, slope_ref, lq1, lk1, lq2, lk2, gain_ref,
                          *rest, kind, pages, n_pages, past, n_new, lam_init):
    k_refs = rest[:pages]
    v_refs = rest[pages:2 * pages]
    o_ref = rest[2 * pages]
    wq_sc, stat_sc, l_sc, acc_sc, live_sc = rest[2 * pages + 1:]
    del pt_ref
    step = pl.program_id(1)
    page = k_refs[0].shape[0]
    rows, d = acc_sc.shape
    row_id = lax.broadcasted_iota(jnp.int32, (rows, 1), 0)
    qpos = past + row_id // REPL
    t2 = _suffix_matrix(LANES) if kind == KIND_SB else None

    def fold(kk, vv, col0, newest):
        n = kk.shape[0]
        s = _dot_nt(wq_sc[...], kk)
        col = col0 + lax.broadcasted_iota(jnp.int32, (1, n), 1)
        if kind == KIND_DIFF:
            dist = (qpos - col).astype(F32)
            sb = s - slope_ref[...] * dist
            if newest:
                sb = jnp.where(dist >= 0.0, sb, NEG_INF)
            m_prev = stat_sc[...]
            m_new = jnp.maximum(m_prev, jnp.max(sb, axis=-1, keepdims=True))
            p = jnp.exp(sb - m_new)
            alpha = jnp.exp(m_prev - m_new)
            l_sc[...] = alpha * l_sc[...] + jnp.sum(p, axis=-1, keepdims=True)
            acc_sc[...] = alpha * acc_sc[...] + jnp.dot(p.astype(BF16), vv, preferred_element_type=F32)
            stat_sc[...] = m_new
        else:
            c = stat_sc[...]
            parts = []
            for hh in reversed(range(n // LANES)):
                z = s[:, hh * LANES:(hh + 1) * LANES]
                lk = -_softplus(z)
                if newest:
                    prior = col[:, hh * LANES:(hh + 1) * LANES] < qpos
                    lk = jnp.where(prior, lk, 0.0)
                later = c + jnp.dot(jnp.concatenate(_split_hi_lo(lk), axis=1), t2, preferred_element_type=F32)
                a = jnp.exp(z + lk + later)
                parts.append(jnp.where(prior, a, 0.0) if newest else a)
                c = c + jnp.sum(lk, axis=-1, keepdims=True)
            a_all = parts[0] if len(parts) == 1 else jnp.concatenate(parts[::-1], axis=1)
            acc_sc[...] = acc_sc[...] + jnp.dot(a_all.astype(BF16), vv, preferred_element_type=F32)
            stat_sc[...] = c

    @pl.when(step == 0)
    def _():
        col = lax.broadcasted_iota(jnp.int32, (1, d), 1)
        wq = jnp.where(col // HEAD_DIM == _row_segment(row_id, kind), qrep_ref[...], 0.0)
        wq_sc[...] = (wq * QK_SCALE).astype(BF16)
        stat_sc[...] = jnp.full(stat_sc.shape, NEG_INF if kind == KIND_DIFF else 0.0, F32)
        l_sc[...] = jnp.zeros(l_sc.shape, F32)
        acc_sc[...] = jnp.zeros(acc_sc.shape, F32)
        live_sc[0] = jnp.int32(1)
        fold(knew_ref[...], vnew_ref[...], past, True)

    def stream():
        kk = jnp.concatenate([k_refs[p][...].astype(BF16) for p in reversed(range(pages))], axis=0)
        vv = jnp.concatenate([v_refs[p][...].astype(BF16) for p in reversed(range(pages))], axis=0)
        fold(kk, vv, (n_pages - (step + 1) * pages) * page, False)

    if kind == KIND_SB:
        @pl.when(live_sc[0] > 0)
        def _():
            stream()
            live_sc[0] = (jnp.max(stat_sc[...]) > SB_LOG_ZERO).astype(jnp.int32)
    else:
        stream()

    @pl.when(step == n_pages // pages - 1)
    def _():
        col = lax.broadcasted_iota(jnp.int32, (1, d), 1)
        if kind == KIND_DIFF:
            lam = _diff_lambda(lq1, lk1, lq2, lk2, lam_init)
            half = REPL // 2
            r = lax.broadcasted_iota(jnp.int32, (half, 1), 0)
            own_cols = col // LANES == r
            for qi in range(n_new):
                lo, mid, hi = qi * REPL, qi * REPL + half, (qi + 1) * REPL
                w = acc_sc[lo:mid, :] / l_sc[lo:mid, :] - lam * (acc_sc[mid:hi, :] / l_sc[mid:hi, :])
                a = jnp.where(own_cols, w, 0.0)
                ms = jnp.sum(a * a, axis=-1, keepdims=True) * (1.0 / LANES)
                a = a * lax.rsqrt(ms + LN_EPS) * gain_ref[...] * (1.0 - lam_init)
                o_ref[qi:qi + 1, :] = jnp.sum(a, axis=0, keepdims=True)
        else:
            r = lax.broadcasted_iota(jnp.int32, (REPL, 1), 0)
            own_cols = col // HEAD_DIM == r
            for qi in range(n_new):
                a = jnp.where(own_cols, acc_sc[qi * REPL:(qi + 1) * REPL, :], 0.0)
                o_ref[qi:qi + 1, :] = jnp.sum(a, axis=0, keepdims=True)


def _sample_stream_mixer(kind, layer, page_table, q, k_new, v_new, cache_k, cache_v, lam_vecs, sub_gain):
    db, n_new, d = q.shape
    n_pages = page_table.shape[1]
    page = cache_k.shape[2]
    past = n_pages * page
    pages = 8 if n_pages % 8 == 0 else 2
    rows = n_new * REPL
    assert page == LANES and n_pages % pages == 0 and n_new <= page and rows % 16 == 0
    lam_init = 0.8 - 0.6 * math.exp(-0.3 * layer)

    qrep = jnp.repeat(q, REPL, axis=1)
    pad = ((0, 0), (0, page - n_new), (0, 0))
    knew = jnp.pad(k_new, pad).astype(BF16)
    vnew = jnp.pad(v_new, pad).astype(BF16)
    slopes = _alibi_slopes(DIFF_HEADS)[(np.arange(rows) % REPL) % DIFF_HEADS]
    slope_rows = jnp.asarray(slopes.reshape(rows, 1))
    gain_full = jnp.tile(sub_gain, (1, d // LANES))

    def per_sample(shape):
        return pl.BlockSpec((None,) + shape, lambda b, s, pt: (b, 0, 0))

    def const(shape):
        return pl.BlockSpec(shape, lambda b, s, pt: (0, 0))

    def page_spec(p):
        return pl.BlockSpec((None, None, page, d),
                            lambda b, s, pt: (layer, pt[b * n_pages + n_pages - 1 - s * pages - p], 0, 0))

    vec = const((1, HEAD_DIM))
    grid_spec = pltpu.PrefetchScalarGridSpec(
        num_scalar_prefetch=1,
        grid=(db, n_pages // pages),
        in_specs=[per_sample((rows, d)), per_sample((page, d)), per_sample((page, d)),
                  const((rows, 1)), vec, vec, vec, vec, const((1, d))]
                 + [page_spec(p) for p in range(pages)] * 2,
        out_specs=per_sample((n_new, d)),
        scratch_shapes=[pltpu.VMEM((rows, d), BF16), pltpu.VMEM((rows, 1), F32), pltpu.VMEM((rows, 1), F32),
                        pltpu.VMEM((rows, d), F32), pltpu.SMEM((1,), jnp.int32)],
    )
    return pl.pallas_call(
        functools.partial(_sample_stream_kernel, kind=kind, pages=pages, n_pages=n_pages, past=past,
                          n_new=n_new, lam_init=lam_init),
        grid_spec=grid_spec,
        out_shape=jax.ShapeDtypeStruct((db, n_new, d), F32),
        compiler_params=pltpu.CompilerParams(vmem_limit_bytes=VMEM_LIMIT,
                                             dimension_semantics=("arbitrary", "arbitrary")),
        name=("diff_sample", "sb_sample")[kind],
    )(page_table.reshape(-1), qrep, knew, vnew, slope_rows, *lam_vecs, gain_full,
      *([cache_k] * pages), *([cache_v] * pages))


def _sample_mixer(kind, layer, page_table, q, k_new, v_new, cache_k, cache_v, lam_vecs, sub_gain):
    if kind != KIND_MOBA:
        return _sample_stream_mixer(kind, layer, page_table, q, k_new, v_new, cache_k, cache_v,
                                    lam_vecs, sub_gain)
    db, n_new, d = q.shape
    n_pages = page_table.shape[1]
    page = cache_k.shape[2]
    past = n_pages * page
    pages = 8 if n_pages % 8 == 0 else 2
    nkc = n_pages // pages
    rows = n_new * REPL
    width = past + page
    assert MOBA_BLOCK == 2 * page and past % MOBA_BLOCK == 0 and past // MOBA_BLOCK < LANES
    assert n_pages % pages == 0 and n_new <= page and rows % 16 == 0
    lam_init = 0.8 - 0.6 * math.exp(-0.3 * layer)

    qrep = jnp.repeat(q, REPL, axis=1)
    pad = ((0, 0), (0, page - n_new), (0, 0))
    knew = jnp.pad(k_new, pad).astype(BF16)
    vnew = jnp.pad(v_new, pad).astype(BF16)
    rep = np.arange(rows) % REPL
    if kind == KIND_DIFF:
        slopes = _alibi_slopes(DIFF_HEADS)[rep % DIFF_HEADS]
    elif kind == KIND_MOBA:
        slopes = _alibi_slopes(MOBA_HEADS)[rep]
    else:
        slopes = np.zeros((rows,), np.float32)
    slope_rows = jnp.asarray(slopes.reshape(rows, 1))
    gain_full = jnp.tile(sub_gain, (1, d // LANES))

    def per_sample(shape):
        return pl.BlockSpec((None,) + shape, lambda b, s, pt: (b, 0, 0))

    def const(shape):
        return pl.BlockSpec(shape, lambda b, s, pt: (0, 0))

    def k_spec(p):
        return pl.BlockSpec(
            (None, None, page, d),
            lambda b, s, pt: (layer, pt[b * n_pages + jnp.minimum(s, nkc - 1) * pages + p], 0, 0))

    def v_spec(p):
        return pl.BlockSpec(
            (None, None, page, d),
            lambda b, s, pt: (layer, pt[b * n_pages + jnp.maximum(s - nkc, 0) * pages + p], 0, 0))

    vec = const((1, HEAD_DIM))
    grid_spec = pltpu.PrefetchScalarGridSpec(
        num_scalar_prefetch=1,
        grid=(db, 2 * nkc),
        in_specs=[per_sample((rows, d)), per_sample((page, d)), per_sample((page, d)),
                  const((rows, 1)), vec, vec, vec, vec, const((1, d))]
                 + [k_spec(p) for p in range(pages)] + [v_spec(p) for p in range(pages)],
        out_specs=per_sample((n_new, d)),
        scratch_shapes=[pltpu.VMEM((rows, d), BF16), pltpu.VMEM((rows, d), F32),
                        pltpu.VMEM((rows, width), F32), pltpu.VMEM((rows, width), F32),
                        pltpu.VMEM((rows, d), F32), pltpu.VMEM((rows, LANES), F32)],
    )
    return pl.pallas_call(
        functools.partial(_sample_kernel, kind=kind, pages=pages, nkc=nkc, past=past,
                          n_new=n_new, lam_init=lam_init),
        grid_spec=grid_spec,
        out_shape=jax.ShapeDtypeStruct((db, n_new, d), F32),
        compiler_params=pltpu.CompilerParams(vmem_limit_bytes=VMEM_LIMIT),
        name=("diff_sample", "sb_sample", "moba_sample")[kind],
    )(page_table.reshape(-1), qrep, knew, vnew, slope_rows, *lam_vecs, gain_full,
      *([cache_k] * pages), *([cache_v] * pages))


def kernel(x_prompt, x_sample, cache_k, cache_v, page_table, w_in, w_out, ln_gain, ln_bias,
           diff_lambda_q1, diff_lambda_k1, diff_lambda_q2, diff_lambda_k2, diff_subln_gain):
    depth = w_in.shape[0]
    b, s, d = x_prompt.shape
    db, ds, _ = x_sample.shape
    alpha = (2 * depth) ** 0.25
    tm_in = min(256, b * s)
    tm_out = min(512, b * s)
    xp = x_prompt.reshape(b * s, d)
    xs = x_sample.reshape(db * ds, d)
    k_all = v_all = None
    ks_rows, vs_rows = [], []
    for i in range(depth):
        kind = i % N_MIXERS
        j = i // N_MIXERS
        lam_vecs = [v[j].reshape(1, HEAD_DIM) for v in
                    (diff_lambda_q1, diff_lambda_k1, diff_lambda_q2, diff_lambda_k2)]
        sub_gain = diff_subln_gain[j].reshape(1, LANES)
        w_in_b = w_in[i].astype(BF16)
        w_out_b = w_out[i].astype(BF16)

        qt, k_all, kb, v_all, vt, g = _in_proj_prompt(xp, w_in_b[:, 0:d].T, w_in_b[:, 2 * d:3 * d].T,
                                                      w_in_b[:, d:4 * d], tm_in, i, depth, k_all, v_all)
        o = _prompt_mixer(kind, i, b, s, qt, k_all.reshape(depth, b, s, d), kb.reshape(b, s, d), vt,
                          lam_vecs, sub_gain)
        xp = _out_proj(o.reshape(b * s, d), g, xp, w_out_b, ln_gain[i], ln_bias[i], alpha, tm_out)

        qs, ks, vs, gs = _in_proj_rows(xs, w_in_b, db * ds)
        sshp = (db, ds, d)
        os_ = _sample_mixer(kind, i, page_table, qs.reshape(sshp), ks.reshape(sshp), vs.reshape(sshp),
                            cache_k, cache_v, lam_vecs, sub_gain)
        xs = _out_proj(os_.reshape(db * ds, d), gs, xs, w_out_b, ln_gain[i], ln_bias[i], alpha, db * ds)
        ks_rows.append(ks.reshape(sshp))
        vs_rows.append(vs.reshape(sshp))
    return (xp.reshape(b, s, d), xs.reshape(db, ds, d), k_all.reshape(depth, b, s, d),
            v_all.reshape(depth, b, s, d), jnp.stack(ks_rows), jnp.stack(vs_rows))
```

```python
import functools
import math

import ml_dtypes
import numpy as np
import jax
import jax.numpy as jnp
from jax import lax
from jax.experimental import pallas as pl
from jax.experimental.pallas import tpu as pltpu

F32 = jnp.float32
BF16 = jnp.bfloat16

N_MIXERS = 3
DIFF_HEADS = 8
HEAD_DIM = 64
SB_HEADS = 16
MOBA_HEADS = 16
MOBA_BLOCK = 256
MOBA_TOPK = 3
LN_EPS = 1e-5
QK_SCALE = HEAD_DIM ** -0.5
LOG2E = 1.4426950408889634
LN2 = 0.6931471805599453

LANES = 128
BF16_ROWS = 16
V_ROWS = LANES + BF16_ROWS
VMEM_LIMIT = 56 * 1024 * 1024

SB_LOG_ZERO = -104.5
ALIBI_LOG_ZERO = -106.0

KIND_DIFF, KIND_SB, KIND_MOBA = 0, 1, 2
NEG_INF = float("-inf")


def _alibi_slopes(n_heads):
    return np.asarray(2.0 ** (-8.0 * np.arange(1, n_heads + 1) / n_heads), dtype=np.float32)


def _slope_table(n_heads):
    s = _alibi_slopes(n_heads)
    rest = (s.astype(np.float64) * LOG2E).astype(np.float32)
    parts = []
    for _ in range(3):
        p = rest.astype(ml_dtypes.bfloat16).astype(np.float32)
        parts.append(p)
        rest = rest - p
    return jnp.asarray(np.stack([s, (1.0 / s).astype(np.float32)] + parts))


def _alibi_key_codes(tk):
    r = lax.broadcasted_iota(jnp.int32, (tk, LANES), 0)
    lane = lax.broadcasted_iota(jnp.int32, (tk, LANES), 1)
    code = jnp.where(lane < 3, r - (r & 1), jnp.where(lane < 6, r & 1, 0))
    return code.astype(F32).astype(BF16)


def _alibi_query_rows(parts, tq):
    row = lax.broadcasted_iota(jnp.int32, (LANES, tq), 0)
    val = jnp.where(row % 3 == 0, parts[0], jnp.where(row % 3 == 1, parts[1], parts[2]))
    return jnp.where(row < 6, val, 0.0).astype(BF16)


def _dot_nt(a, b):
    return lax.dot_general(a, b, (((1,), (1,)), ((), ())), preferred_element_type=F32)


def _diff_lambda(lq1, lk1, lq2, lk2, lam_init):
    a = jnp.exp(jnp.sum(lq1[...] * lk1[...], axis=-1, keepdims=True))
    b = jnp.exp(jnp.sum(lq2[...] * lk2[...], axis=-1, keepdims=True))
    return a - b + lam_init


def _top_select(gate, valid, axis):
    pos = lax.broadcasted_iota(jnp.int32, gate.shape, axis).astype(F32)
    g = jnp.where(valid, gate, NEG_INF)
    sel = jnp.zeros(gate.shape, F32)
    for _ in range(MOBA_TOPK):
        m = jnp.max(g, axis=axis, keepdims=True)
        idx = jnp.min(jnp.where(g == m, pos, 1e9), axis=axis, keepdims=True)
        idx = jnp.where(m > NEG_INF, idx, 1e9)
        pick = pos == idx
        sel = jnp.where(pick, 1.0, sel)
        g = jnp.where(pick, NEG_INF, g)
    return sel


def _softplus(z):
    return jnp.maximum(z, 0.0) + jnp.log1p(jnp.exp(-jnp.abs(z)))


def _split_hi_lo(x):
    hi = x.astype(BF16)
    return hi, (x - hi.astype(F32)).astype(BF16)


def _in_proj_rows_kernel(x_ref, w_ref, q_ref, k_ref, v_ref, g_ref):
    xb = x_ref[...].astype(BF16)
    d = q_ref.shape[-1]
    for j, o_ref in enumerate((q_ref, k_ref, v_ref, g_ref)):
        o_ref[...] = jnp.dot(xb, w_ref[:, j * d:(j + 1) * d], preferred_element_type=F32)


def _in_proj_rows(x2d, w_bf16, tm):
    n, d = x2d.shape
    row = pl.BlockSpec((tm, d), lambda i: (i, 0))
    return pl.pallas_call(
        _in_proj_rows_kernel,
        grid=(n // tm,),
        in_specs=[row, pl.BlockSpec((d, 4 * d), lambda i: (0, 0))],
        out_specs=[row] * 4,
        out_shape=[jax.ShapeDtypeStruct((n, d), F32)] * 4,
        compiler_params=pltpu.CompilerParams(vmem_limit_bytes=VMEM_LIMIT),
        name="in_proj_rows",
    )(x2d, w_bf16)


def _in_proj_prompt_kernel(x_ref, wqt_ref, wvt_ref, w_ref, k_all_ref, v_all_ref,
                           qt_ref, k_ref, kb_ref, v_ref, vt_ref, g_ref):
    del k_all_ref, v_all_ref
    xb = x_ref[...].astype(BF16)
    d = k_ref.shape[-1]
    qt = _dot_nt(wqt_ref[...], xb)
    vt = _dot_nt(wvt_ref[...], xb).astype(BF16)
    ones = jnp.ones((V_ROWS - LANES, vt.shape[1]), BF16)
    for gi in range(d // LANES):
        qt_ref[gi] = qt[gi * LANES:(gi + 1) * LANES, :]
        vt_ref[gi, 0:LANES, :] = vt[gi * LANES:(gi + 1) * LANES, :]
        vt_ref[gi, LANES:V_ROWS, :] = ones
    k = jnp.dot(xb, w_ref[:, 0:d], preferred_element_type=F32)
    k_ref[...] = k
    kb_ref[...] = k.astype(BF16)
    v_ref[...] = jnp.dot(xb, w_ref[:, d:2 * d], preferred_element_type=F32)
    g_ref[...] = jnp.dot(xb, w_ref[:, 2 * d:3 * d], preferred_element_type=F32)


def _in_proj_prompt(x2d, wqt, wvt, w_kvg, tm, layer, depth, k_all, v_all):
    n, d = x2d.shape
    groups = d // LANES
    row = pl.BlockSpec((tm, d), lambda i: (i, 0))
    sq = pl.BlockSpec((d, d), lambda i: (0, 0))
    layer_row = pl.BlockSpec((None, tm, d), lambda i: (layer, i, 0))
    stack = jax.ShapeDtypeStruct((depth, n, d), F32)
    hbm = pl.BlockSpec(memory_space=pl.ANY)
    return pl.pallas_call(
        _in_proj_prompt_kernel,
        grid=(n // tm,),
        in_specs=[row, sq, sq, pl.BlockSpec((d, 3 * d), lambda i: (0, 0)), hbm, hbm],
        out_specs=[pl.BlockSpec((groups, LANES, tm), lambda i: (0, 0, i)), layer_row, row, layer_row,
                   pl.BlockSpec((groups, V_ROWS, tm), lambda i: (0, 0, i)), row],
        out_shape=[jax.ShapeDtypeStruct((groups, LANES, n), F32), stack, jax.ShapeDtypeStruct((n, d), BF16),
                   stack, jax.ShapeDtypeStruct((groups, V_ROWS, n), BF16), jax.ShapeDtypeStruct((n, d), F32)],
        input_output_aliases={4: 1, 5: 3},
        compiler_params=pltpu.CompilerParams(vmem_limit_bytes=VMEM_LIMIT),
        name="in_proj_prompt",
    )(x2d, wqt, wvt, w_kvg, k_all, v_all)


def _out_proj_kernel(o_ref, g_ref, x_ref, w_ref, gain_ref, bias_ref, y_ref, *, alpha):
    gt = g_ref[...]
    og = o_ref[...] * (gt / (1.0 + jnp.exp(-gt)))
    y = jnp.dot(og.astype(BF16), w_ref[...], preferred_element_type=F32)
    z = alpha * x_ref[...] + y
    mu = jnp.mean(z, axis=-1, keepdims=True)
    zc = z - mu
    var = jnp.mean(zc * zc, axis=-1, keepdims=True)
    y_ref[...] = zc * lax.rsqrt(var + LN_EPS) * gain_ref[...] + bias_ref[...]


def _out_proj(o2d, g2d, x2d, w_bf16, gain, bias, alpha, tm):
    n, d = x2d.shape
    row = pl.BlockSpec((tm, d), lambda i: (i, 0))
    vec = pl.BlockSpec((1, d), lambda i: (0, 0))
    return pl.pallas_call(
        functools.partial(_out_proj_kernel, alpha=alpha),
        grid=(n // tm,),
        in_specs=[row, row, row, pl.BlockSpec((d, d), lambda i: (0, 0)), vec, vec],
        out_specs=row,
        out_shape=jax.ShapeDtypeStruct((n, d), F32),
        compiler_params=pltpu.CompilerParams(vmem_limit_bytes=VMEM_LIMIT),
        name="out_proj_ln",
    )(o2d, g2d, x2d, w_bf16, gain.reshape(1, d), bias.reshape(1, d))


def _split_maps_t(qt):
    row = lax.broadcasted_iota(jnp.int32, qt.shape, 0)
    return [jnp.where(row < HEAD_DIM, qt, 0.0), jnp.where(row >= HEAD_DIM, qt, 0.0)]


def _key_abs_max(k_ref, seq, chunk):
    def body(it, acc):
        kc = k_ref[pl.ds(pl.multiple_of(it * chunk, chunk), chunk), :].astype(F32)
        return jnp.maximum(acc, jnp.max(jnp.abs(kc), axis=0, keepdims=True))

    kmax = lax.fori_loop(0, seq // chunk, body, jnp.zeros((1, LANES), F32))
    return jnp.broadcast_to(kmax, (BF16_ROWS, LANES)).astype(BF16)


def _score_bound(kabs, qm_f32):
    return jnp.dot(kabs, jnp.abs(qm_f32).astype(BF16), preferred_element_type=F32)[0:1]


def _first_live_tile(excess, slope_inv, q0, tk, hi):
    e = jnp.max(excess, axis=-1, keepdims=True) * LN2
    lim = ((ALIBI_LOG_ZERO - e) * slope_inv + (q0 - tk + 1).astype(F32)) * (1.0 / tk)
    j_lo = jnp.clip(jnp.ceil(lim), 0.0, hi.astype(F32)).astype(jnp.int32)
    return jnp.max(j_lo)


def _softmax_tile(u_ref, m_tile, shift, m_prev, vt, acc_ref):
    m_new = jnp.maximum(m_prev, m_tile + shift)
    m_fin = jnp.where(m_new > NEG_INF, m_new, 0.0)
    p = jnp.exp2(u_ref[...] - (m_fin - shift)).astype(BF16)
    pv = jnp.dot(vt, p, preferred_element_type=F32)
    acc_ref[...] = jnp.exp2(m_prev - m_fin) * acc_ref[...] + pv
    return m_new


def _pipelined_sweep(n_diag, first_diag, scores, update, first_live, m_init):
    m = m_init
    mt_cur = scores(first_diag, 0, 0)
    j_lo = None
    last_past = jnp.maximum(first_diag - 1, 0)
    for n in range(n_diag):
        if n + 1 < n_diag:
            mt_nxt = scores(first_diag + n + 1, (n + 1) & 1, n + 1)
        else:
            j_lo = first_live(m, mt_cur, n)
            mt_nxt = scores(jnp.minimum(j_lo, last_past), n_diag & 1, None)
        m = update(first_diag + n, n & 1, mt_cur, m)
        mt_cur = mt_nxt

    slot_a = n_diag & 1
    slot_b = slot_a ^ 1
    n_past = first_diag - j_lo

    def body(pair, carry):
        m_prev, mt_a = carry
        a = j_lo + 2 * pair
        mt_b = scores(a + 1, slot_b, None)
        m_mid = update(a, slot_a, mt_a, m_prev)
        mt_next = scores(jnp.minimum(a + 2, last_past), slot_a, None)
        return update(a + 1, slot_b, mt_b, m_mid), mt_next

    m, mt_cur = lax.fori_loop(0, n_past // 2, body, (m, mt_cur))

    @pl.when(n_past % 2 == 1)
    def _():
        update(last_past, slot_a, mt_cur, m)


def _diff_prompt_kernel(slope_ref, qt_ref, k_ref, vt_ref, lq1, lk1, lq2, lk2, gain_ref, o_ref,
                        kcode_sc, kabs_sc, acc_sc, u_sc, *, t, seq, lam_init):
    g = pl.program_id(1)
    i = pl.program_id(2)
    slope2 = slope_ref[0, g] * LOG2E

    @pl.when(i == 0)
    def _():
        kcode_sc[...] = _alibi_key_codes(t)
        kabs_sc[...] = _key_abs_max(k_ref, seq, t)

    qf = _split_maps_t(qt_ref[...] * (QK_SCALE * LOG2E))
    slope_rows = _alibi_query_rows([slope_ref[2 + e, g] for e in range(3)], t)
    qm = [jnp.concatenate([x.astype(BF16), slope_rows], axis=0) for x in qf]
    q0 = i * t
    acc_sc[...] = jnp.zeros(acc_sc.shape, F32)

    def scores(j, slot, diag_n):
        kt = jnp.concatenate([k_ref[pl.ds(pl.multiple_of(j * t, t), t), :], kcode_sc[...]], axis=1)
        maxima = []
        for c in range(2):
            u = jnp.dot(kt, qm[c], preferred_element_type=F32)
            if diag_n is not None:
                krow = lax.broadcasted_iota(jnp.int32, (t, t), 0)
                qcol = lax.broadcasted_iota(jnp.int32, (t, t), 1)
                u = jnp.where(krow <= qcol, u, NEG_INF)
            u_sc[slot, c] = u
            maxima.append(jnp.max(u, axis=0, keepdims=True))
        return tuple(maxima)

    def update(j, slot, maxima, m_prev):
        k0 = pl.multiple_of(j * t, t)
        vt = vt_ref[:, pl.ds(k0, t)]
        shift = slope2 * (jnp.zeros((1, t), jnp.int32) + (k0 - q0)).astype(F32)
        return tuple(_softmax_tile(u_sc.at[slot, c], maxima[c], shift, m_prev[c], vt, acc_sc.at[c])
                     for c in range(2))

    def first_live(m_prev, maxima, n):
        del m_prev, n
        kabs = kabs_sc[...]
        excess = jnp.maximum(_score_bound(kabs, qf[0]) - maxima[0], _score_bound(kabs, qf[1]) - maxima[1])
        return _first_live_tile(excess, slope_ref[1, g], q0, t, i)

    neg = jnp.full((1, t), NEG_INF, F32)
    _pipelined_sweep(1, i, scores, update, first_live, (neg, neg))

    lam = _diff_lambda(lq1, lk1, lq2, lk2, lam_init)
    o = (acc_sc[0, 0:LANES, :] / acc_sc[0, LANES:LANES + 1, :]
         - lam * (acc_sc[1, 0:LANES, :] / acc_sc[1, LANES:LANES + 1, :]))
    ms = jnp.mean(o * o, axis=0, keepdims=True)
    o_ref[...] = (o * lax.rsqrt(ms + LN_EPS)).T * gain_ref[...] * (1.0 - lam_init)


def _moba_prompt_kernel(slope_ref, qt_ref, k_ref, vt_ref, mean_ref, o_ref,
                        kcode_sc, kabs_sc, acc_sc, sel_sc, u_sc, *, tq, seq):
    tk = MOBA_BLOCK
    n_diag = tq // tk
    g = pl.program_id(1)
    i = pl.program_id(2)
    slopes = [slope_ref[0, 2 * g] * LOG2E, slope_ref[0, 2 * g + 1] * LOG2E]

    @pl.when(i == 0)
    def _():
        kcode_sc[...] = _alibi_key_codes(tk)
        kabs_sc[...] = _key_abs_max(k_ref, seq, tk)

    qf = _split_maps_t(qt_ref[...])
    qs = [x * (QK_SCALE * LOG2E) for x in qf]
    qm = [jnp.concatenate([qs[h].astype(BF16),
                           _alibi_query_rows([slope_ref[2 + e, 2 * g + h] for e in range(3)], tq)], axis=0)
          for h in range(2)]
    means = mean_ref[...]
    first_diag = i * n_diag
    col_blk = lax.broadcasted_iota(jnp.int32, (1, tq), 1) // tk
    blk = lax.broadcasted_iota(jnp.int32, (means.shape[0], tq), 0)
    for h in range(2):
        gate = jnp.dot(means, qf[h], precision=lax.Precision.HIGHEST, preferred_element_type=F32)
        sel_sc[h] = _top_select(gate, blk < first_diag + col_blk, axis=0)
    q0 = i * tq
    acc_sc[...] = jnp.zeros(acc_sc.shape, F32)

    def scores(j, slot, diag_n):
        kt = jnp.concatenate([k_ref[pl.ds(pl.multiple_of(j * tk, tk), tk), :], kcode_sc[...]], axis=1)
        maxima = []
        for h in range(2):
            u = jnp.dot(kt, qm[h], preferred_element_type=F32)
            picked = sel_sc[h, pl.ds(j, 1), :]
            if diag_n is None:
                u = jnp.where(picked > 0.0, u, NEG_INF)
            else:
                allowed = jnp.where(col_blk == diag_n, 1.0, picked)
                krow = lax.broadcasted_iota(jnp.int32, (tk, tq), 0) + diag_n * tk
                qcol = lax.broadcasted_iota(jnp.int32, (tk, tq), 1)
                u = jnp.where(krow <= qcol, jnp.where(allowed > 0.0, u, NEG_INF), NEG_INF)
            u_sc[slot, h] = u
            maxima.append(jnp.max(u, axis=0, keepdims=True))
        return tuple(maxima)

    def update(j, slot, maxima, m_prev):
        k0 = pl.multiple_of(j * tk, tk)
        vt = vt_ref[:, pl.ds(k0, tk)]
        rel = (jnp.zeros((1, tq), jnp.int32) + (k0 - q0)).astype(F32)
        return tuple(_softmax_tile(u_sc.at[slot, h], maxima[h], slopes[h] * rel, m_prev[h], vt, acc_sc.at[h])
                     for h in range(2))

    def first_live(m_prev, maxima, n):
        kabs = kabs_sc[...]
        j_lo = None
        for h in range(2):
            m_h = jnp.maximum(m_prev[h], maxima[h] + slopes[h] * float(n * tk))
            j_h = _first_live_tile(_score_bound(kabs, qs[h]) - m_h, slope_ref[1, 2 * g + h], q0, tk, first_diag)
            j_lo = j_h if j_lo is None else jnp.minimum(j_lo, j_h)
        return j_lo

    neg = jnp.full((1, tq), NEG_INF, F32)
    _pipelined_sweep(n_diag, first_diag, scores, update, first_live, (neg, neg))

    t = tq
    row = lax.broadcasted_iota(jnp.int32, (LANES, t), 0)
    o = jnp.where(row < HEAD_DIM, acc_sc[0, 0:LANES, :] / acc_sc[0, LANES:LANES + 1, :],
                  acc_sc[1, 0:LANES, :] / acc_sc[1, LANES:LANES + 1, :])
    o_ref[...] = o.T


def _suffix_matrix_t(tk):
    s = lax.broadcasted_iota(jnp.int32, (tk + BF16_ROWS, 2 * tk), 0)
    j = lax.broadcasted_iota(jnp.int32, (tk + BF16_ROWS, 2 * tk), 1)
    return jnp.where(s >= tk, 1.0, jnp.where((j & (tk - 1)) > s, 1.0, 0.0)).astype(BF16)


def _sb_prompt_kernel(*refs, batch, seq, tq, tk):
    qt_refs = refs[:batch]
    k_ref, vt_ref, o_ref, acc_sc = refs[batch:]
    i = pl.program_id(1)
    chains = [(b, h) for b in range(batch) for h in range(2)]
    qm = {}
    for b in range(batch):
        for h, x in enumerate(_split_maps_t(qt_refs[b][...] * QK_SCALE)):
            qm[b, h] = x.astype(BF16)
    acc_sc[...] = jnp.zeros(acc_sc.shape, F32)
    qpos = i * tq + lax.broadcasted_iota(jnp.int32, (tk, tq), 1)
    krow = lax.broadcasted_iota(jnp.int32, (tk, tq), 0)
    tmat = _suffix_matrix_t(tk)

    def chunk(j, c_prev):
        k0 = pl.multiple_of(j * tk, tk)
        kt = [k_ref[b, pl.ds(k0, tk), :] for b in range(batch)]
        vt = [vt_ref[0:LANES, pl.ds(pl.multiple_of(b * seq + k0, tk), tk)] for b in range(batch)]
        past = (k0 + krow) < qpos
        z = [jnp.dot(kt[b], qm[b, h], preferred_element_type=F32) for b, h in chains]
        lk = [jnp.where(past, -_softplus(zz), 0.0) for zz in z]
        suf = [jnp.dot(tmat, jnp.concatenate(_split_hi_lo(x), axis=0), preferred_element_type=F32) for x in lk]
        c_new = []
        for n, (b, h) in enumerate(chains):
            later = c_prev[n] + suf[n][0:tk]
            a = jnp.where(past, jnp.exp(z[n] + lk[n] + later), 0.0)
            acc_sc[b, h] = acc_sc[b, h] + jnp.dot(vt[b], a.astype(BF16), preferred_element_type=F32)
            c_new.append(c_prev[n] + suf[n][tk:tk + 1])
        return tuple(c_new)

    def cond(carry):
        return jnp.logical_and(carry[0] >= 0, carry[1] > 0)

    def body(carry):
        c = chunk(carry[0], carry[2:])
        c_max = c[0]
        for x in c[1:]:
            c_max = jnp.maximum(c_max, x)
        live = jnp.max(c_max) > SB_LOG_ZERO
        return (carry[0] - 1, live.astype(jnp.int32)) + c

    zero = jnp.zeros((1, tq), F32)
    lax.while_loop(cond, body, (i * (tq // tk) + tq // tk - 1, jnp.int32(1)) + (zero,) * len(chains))
    row = lax.broadcasted_iota(jnp.int32, (LANES, tq), 0)
    for b in range(batch):
        o_ref[b] = jnp.where(row < HEAD_DIM, acc_sc[b, 0], acc_sc[b, 1]).T


def _block_mean_kernel(k_ref, o_ref, *, per_step):
    k = k_ref[...]
    o_ref[...] = jnp.mean(k.reshape(per_step, MOBA_BLOCK, k.shape[-1]), axis=1)


def _block_means(k_all, layer):
    _, b, s, d = k_all.shape
    nb = s // MOBA_BLOCK
    per_step = 8 if nb % 8 == 0 else nb
    return pl.pallas_call(
        functools.partial(_block_mean_kernel, per_step=per_step),
        grid=(b, nb // per_step),
        in_specs=[pl.BlockSpec((None, None, per_step * MOBA_BLOCK, d), lambda bi, n: (layer, bi, n, 0))],
        out_specs=pl.BlockSpec((None, per_step, d), lambda bi, n: (bi, n, 0)),
        out_shape=jax.ShapeDtypeStruct((b, nb, d), F32),
        compiler_params=pltpu.CompilerParams(vmem_limit_bytes=VMEM_LIMIT),
        name="moba_block_means",
    )(k_all)


def _prompt_mixer(kind, layer, b, s, qt, k, kb, vt, lam_vecs, sub_gain):
    d = k.shape[-1]
    groups = d // LANES
    smem = pl.BlockSpec(memory_space=pltpu.SMEM)
    k_spec = pl.BlockSpec((None, s, LANES), lambda bi, g, i: (bi, 0, g))
    vt_spec = pl.BlockSpec((None, V_ROWS, s), lambda bi, g, i: (g, 0, bi))

    def q_spec(t):
        return pl.BlockSpec((None, LANES, t), lambda bi, g, i: (g, 0, bi * (s // t) + i))

    def o_spec(t):
        return pl.BlockSpec((None, t, LANES), lambda bi, g, i: (bi, i, g))

    params = pltpu.CompilerParams(vmem_limit_bytes=VMEM_LIMIT,
                                  dimension_semantics=("arbitrary", "arbitrary", "arbitrary"))
    out_shape = jax.ShapeDtypeStruct((b, s, d), F32)
    if kind == KIND_DIFF:
        t = min(512, s)
        lam_init = 0.8 - 0.6 * math.exp(-0.3 * layer)
        vec = pl.BlockSpec((1, HEAD_DIM), lambda bi, g, i: (0, 0))
        return pl.pallas_call(
            functools.partial(_diff_prompt_kernel, t=t, seq=s, lam_init=lam_init),
            grid=(b, groups, s // t),
            in_specs=[smem, q_spec(t), k_spec, vt_spec, vec, vec, vec, vec,
                      pl.BlockSpec((1, LANES), lambda bi, g, i: (0, 0))],
            out_specs=o_spec(t),
            out_shape=out_shape,
            scratch_shapes=[pltpu.VMEM((t, LANES), BF16), pltpu.VMEM((BF16_ROWS, LANES), BF16),
                            pltpu.VMEM((2, V_ROWS, t), F32), pltpu.VMEM((2, 2, t, t), F32)],
            compiler_params=params,
            name="diff_prompt",
        )(_slope_table(DIFF_HEADS), qt, kb, vt, *lam_vecs, sub_gain)
    if kind == KIND_SB:
        tq, tk = min(256, s), 128
        nq = s // tq
        return pl.pallas_call(
            functools.partial(_sb_prompt_kernel, batch=b, seq=s, tq=tq, tk=tk),
            grid=(groups, nq),
            in_specs=[pl.BlockSpec((None, LANES, tq), lambda g, i, bb=bb: (g, 0, bb * nq + i)) for bb in range(b)]
                     + [pl.BlockSpec((b, s, LANES), lambda g, i: (0, 0, g)),
                        pl.BlockSpec((None, V_ROWS, b * s), lambda g, i: (g, 0, 0))],
            out_specs=pl.BlockSpec((b, tq, LANES), lambda g, i: (0, i, g)),
            out_shape=out_shape,
            scratch_shapes=[pltpu.VMEM((b, 2, LANES, tq), F32)],
            compiler_params=pltpu.CompilerParams(vmem_limit_bytes=VMEM_LIMIT,
                                                 dimension_semantics=("arbitrary", "arbitrary")),
            name="sb_prompt",
        )(*([qt] * b), kb, vt)
    tq, tk = min(2 * MOBA_BLOCK, s), MOBA_BLOCK
    nb = s // tk
    means = _block_means(k, layer)
    return pl.pallas_call(
        functools.partial(_moba_prompt_kernel, tq=tq, seq=s),
        grid=(b, groups, s // tq),
        in_specs=[smem, q_spec(tq), k_spec, vt_spec,
                  pl.BlockSpec((None, nb, LANES), lambda bi, g, i: (bi, 0, g))],
        out_specs=o_spec(tq),
        out_shape=out_shape,
        scratch_shapes=[pltpu.VMEM((tk, LANES), BF16), pltpu.VMEM((BF16_ROWS, LANES), BF16),
                        pltpu.VMEM((2, V_ROWS, tq), F32), pltpu.VMEM((2, nb, tq), F32),
                        pltpu.VMEM((2, 2, tk, tq), F32)],
        compiler_params=params,
        name="moba_prompt",
    )(_slope_table(MOBA_HEADS), qt, kb, vt, means)


REPL = 16


def _row_segment(r, kind):
    rep = r % REPL
    if kind == KIND_DIFF:
        return 2 * (rep % DIFF_HEADS) + rep // DIFF_HEADS
    return rep


def _suffix_matrix(tk):
    j = lax.broadcasted_iota(jnp.int32, (2 * tk, tk), 0)
    s = lax.broadcasted_iota(jnp.int32, (2 * tk, tk), 1)
    return jnp.where((j & (tk - 1)) > s, 1.0, 0.0).astype(BF16)


def _sample_kernel(pt_ref, qrep_ref, knew_ref, vnew_ref, slope_ref, lq1, lk1, lq2, lk2, gain_ref,
                   *rest, kind, pages, nkc, past, n_new, lam_init):
    k_refs = rest[:pages]
    v_refs = rest[pages:2 * pages]
    o_ref = rest[2 * pages]
    wq_sc, wqf_sc, s_sc, p_sc, acc_sc, gate_sc = rest[2 * pages + 1:]
    del pt_ref
    step = pl.program_id(1)
    page = k_refs[0].shape[0]
    rows, d = wqf_sc.shape
    width = s_sc.shape[1]
    out_rows = rows // 2 if kind == KIND_DIFF else rows
    row_id = lax.broadcasted_iota(jnp.int32, (rows, 1), 0)
    qpos = past + row_id // REPL

    @pl.when(step == 0)
    def _():
        col = lax.broadcasted_iota(jnp.int32, (1, d), 1)
        wq = jnp.where(col // HEAD_DIM == _row_segment(row_id, kind), qrep_ref[...], 0.0)
        wqf_sc[...] = wq
        wq_sc[...] = (wq * QK_SCALE).astype(BF16)
        gate_sc[...] = jnp.zeros(gate_sc.shape, F32)

    @pl.when(step < nkc)
    def _():
        for pp in range(0, pages, 2):
            k0 = k_refs[pp][...]
            k1 = k_refs[pp + 1][...]
            kk = jnp.concatenate([k0.astype(BF16), k1.astype(BF16)], axis=0)
            col0 = pl.multiple_of((step * pages + pp) * page, 2 * page)
            s_sc[:, pl.ds(col0, 2 * page)] = _dot_nt(wq_sc[...], kk)
            if kind == KIND_MOBA:
                mean = (jnp.sum(k0, axis=0, keepdims=True) + jnp.sum(k1, axis=0, keepdims=True)) * (1.0 / MOBA_BLOCK)
                gcol = jnp.sum(wqf_sc[...] * mean, axis=-1, keepdims=True)
                lane = lax.broadcasted_iota(jnp.int32, gate_sc.shape, 1)
                gate_sc[...] = jnp.where(lane == (step * pages + pp) // 2, gcol, gate_sc[...])

    @pl.when(step == nkc)
    def _():
        s_sc[:, past:past + page] = _dot_nt(wq_sc[...], knew_ref[...])
        col = lax.broadcasted_iota(jnp.int32, (1, width), 1)
        if kind == KIND_SB:
            t2 = _suffix_matrix(page)
            n_chunks = width // page

            def body(it, c_prev):
                c0 = pl.multiple_of((n_chunks - 1 - it) * page, page)
                z = s_sc[:, pl.ds(c0, page)]
                colpos = c0 + lax.broadcasted_iota(jnp.int32, (1, page), 1)
                prior = colpos < qpos
                lk = jnp.where(prior, -_softplus(z), 0.0)
                hi, lo = _split_hi_lo(lk)
                later = c_prev + jnp.dot(jnp.concatenate([hi, lo], axis=1), t2, preferred_element_type=F32)
                p_sc[:, pl.ds(c0, page)] = jnp.where(prior, jnp.exp(z + lk + later), 0.0)
                return c_prev + jnp.sum(lk, axis=-1, keepdims=True)

            lax.fori_loop(0, n_chunks, body, jnp.zeros((rows, 1), F32))
        else:
            dist = (qpos - col).astype(F32)
            ok = dist >= 0.0
            if kind == KIND_MOBA:
                own = past // MOBA_BLOCK
                lane = lax.broadcasted_iota(jnp.int32, gate_sc.shape, 1)
                sel = _top_select(gate_sc[...], lane < own, axis=1)
                sel = jnp.where(lane == own, 1.0, sel)
                blk_row = lax.broadcasted_iota(jnp.int32, (gate_sc.shape[1], 1), 0)
                expand = jnp.where(col // MOBA_BLOCK == blk_row, 1.0, 0.0).astype(BF16)
                picked = jnp.dot(sel.astype(BF16), expand, preferred_element_type=F32)
                ok = jnp.logical_and(ok, picked > 0.5)
            sb = jnp.where(ok, s_sc[...] - slope_ref[...] * dist, NEG_INF)
            e = jnp.exp(sb - jnp.max(sb, axis=-1, keepdims=True))
            p = e / jnp.sum(e, axis=-1, keepdims=True)
            if kind == KIND_DIFF:
                lam = _diff_lambda(lq1, lk1, lq2, lk2, lam_init)
                half = REPL // 2
                for qi in range(n_new):
                    p_sc[qi * half:(qi + 1) * half, :] = (
                        p[qi * REPL:qi * REPL + half] - lam * p[qi * REPL + half:(qi + 1) * REPL])
            else:
                p_sc[...] = p
        acc_sc[0:out_rows, :] = jnp.dot(p_sc[0:out_rows, past:past + page].astype(BF16), vnew_ref[...],
                                        preferred_element_type=F32)

    @pl.when(step >= nkc)
    def _():
        for pp in range(0, pages, 2):
            vv = jnp.concatenate([v_refs[pp][...].astype(BF16), v_refs[pp + 1][...].astype(BF16)], axis=0)
            col0 = pl.multiple_of(((step - nkc) * pages + pp) * page, 2 * page)
            w = p_sc[0:out_rows, pl.ds(col0, 2 * page)].astype(BF16)
            acc_sc[0:out_rows, :] = acc_sc[0:out_rows, :] + jnp.dot(w, vv, preferred_element_type=F32)

    @pl.when(step == 2 * nkc - 1)
    def _():
        per_q = out_rows // n_new
        seg_w = LANES if kind == KIND_DIFF else HEAD_DIM
        r = lax.broadcasted_iota(jnp.int32, (per_q, 1), 0)
        col = lax.broadcasted_iota(jnp.int32, (1, d), 1)
        own_cols = col // seg_w == r
        for qi in range(n_new):
            a = jnp.where(own_cols, acc_sc[qi * per_q:(qi + 1) * per_q, :], 0.0)
            if kind == KIND_DIFF:
                ms = jnp.sum(a * a, axis=-1, keepdims=True) * (1.0 / LANES)
                a = a * lax.rsqrt(ms + LN_EPS) * gain_ref[...] * (1.0 - lam_init)
            o_ref[qi:qi + 1, :] = jnp.sum(a, axis=0, keepdims=True)


def _sample_stream_kernel(pt_ref, qrep_ref, knew_ref, vnew_ref, slope_ref, lq1, lk1, lq2, lk2, gain_ref,
                          *rest, kind, pages, n_pages, past, n_new, lam_init):
    k_refs = rest[:pages]
    v_refs = rest[pages:2 * pages]
    o_ref = rest[2 * pages]
    wq_sc, stat_sc, l_sc, acc_sc, live_sc = rest[2 * pages + 1:]
    del pt_ref
    step = pl.program_id(1)
    page = k_refs[0].shape[0]
    rows, d = acc_sc.shape
    row_id = lax.broadcasted_iota(jnp.int32, (rows, 1), 0)
    qpos = past + row_id // REPL
    t2 = _suffix_matrix(LANES) if kind == KIND_SB else None

    def fold(kk, vv, col0, newest):
        n = kk.shape[0]
        s = _dot_nt(wq_sc[...], kk)
        col = col0 + lax.broadcasted_iota(jnp.int32, (1, n), 1)
        if kind == KIND_DIFF:
            dist = (qpos - col).astype(F32)
            sb = s - slope_ref[...] * dist
            if newest:
                sb = jnp.where(dist >= 0.0, sb, NEG_INF)
            m_prev = stat_sc[...]
            m_new = jnp.maximum(m_prev, jnp.max(sb, axis=-1, keepdims=True))
            p = jnp.exp(sb - m_new)
            alpha = jnp.exp(m_prev - m_new)
            l_sc[...] = alpha * l_sc[...] + jnp.sum(p, axis=-1, keepdims=True)
            acc_sc[...] = alpha * acc_sc[...] + jnp.dot(p.astype(BF16), vv, preferred_element_type=F32)
            stat_sc[...] = m_new
        else:
            c = stat_sc[...]
            parts = []
            for hh in reversed(range(n // LANES)):
                z = s[:, hh * LANES:(hh + 1) * LANES]
                lk = -_softplus(z)
                if newest:
                    prior = col[:, hh * LANES:(hh + 1) * LANES] < qpos
                    lk = jnp.where(prior, lk, 0.0)
                later = c + jnp.dot(jnp.concatenate(_split_hi_lo(lk), axis=1), t2, preferred_element_type=F32)
                a = jnp.exp(z + lk + later)
                parts.append(jnp.where(prior, a, 0.0) if newest else a)
                c = c + jnp.sum(lk, axis=-1, keepdims=True)
            a_all = parts[0] if len(parts) == 1 else jnp.concatenate(parts[::-1], axis=1)
            acc_sc[...] = acc_sc[...] + jnp.dot(a_all.astype(BF16), vv, preferred_element_type=F32)
            stat_sc[...] = c

    @pl.when(step == 0)
    def _():
        col = lax.broadcasted_iota(jnp.int32, (1, d), 1)
        wq = jnp.where(col // HEAD_DIM == _row_segment(row_id, kind), qrep_ref[...], 0.0)
        wq_sc[...] = (wq * QK_SCALE).astype(BF16)
        stat_sc[...] = jnp.full(stat_sc.shape, NEG_INF if kind == KIND_DIFF else 0.0, F32)
        l_sc[...] = jnp.zeros(l_sc.shape, F32)
        acc_sc[...] = jnp.zeros(acc_sc.shape, F32)
        live_sc[0] = jnp.int32(1)
        fold(knew_ref[...], vnew_ref[...], past, True)

    def stream():
        kk = jnp.concatenate([k_refs[p][...].astype(BF16) for p in reversed(range(pages))], axis=0)
        vv = jnp.concatenate([v_refs[p][...].astype(BF16) for p in reversed(range(pages))], axis=0)
        fold(kk, vv, (n_pages - (step + 1) * pages) * page, False)

    if kind == KIND_SB:
        @pl.when(live_sc[0] > 0)
        def _():
            stream()
            live_sc[0] = (jnp.max(stat_sc[...]) > SB_LOG_ZERO).astype(jnp.int32)
    else:
        stream()

    @pl.when(step == n_pages // pages - 1)
    def _():
        col = lax.broadcasted_iota(jnp.int32, (1, d), 1)
        if kind == KIND_DIFF:
            lam = _diff_lambda(lq1, lk1, lq2, lk2, lam_init)
            half = REPL // 2
            r = lax.broadcasted_iota(jnp.int32, (half, 1), 0)
            own_cols = col // LANES == r
            for qi in range(n_new):
                lo, mid, hi = qi * REPL, qi * REPL + half, (qi + 1) * REPL
                w = acc_sc[lo:mid, :] / l_sc[lo:mid, :] - lam * (acc_sc[mid:hi, :] / l_sc[mid:hi, :])
                a = jnp.where(own_cols, w, 0.0)
                ms = jnp.sum(a * a, axis=-1, keepdims=True) * (1.0 / LANES)
                a = a * lax.rsqrt(ms + LN_EPS) * gain_ref[...] * (1.0 - lam_init)
                o_ref[qi:qi + 1, :] = jnp.sum(a, axis=0, keepdims=True)
        else:
            r = lax.broadcasted_iota(jnp.int32, (REPL, 1), 0)
            own_cols = col // HEAD_DIM == r
            for qi in range(n_new):
                a = jnp.where(own_cols, acc_sc[qi * REPL:(qi + 1) * REPL, :], 0.0)
                o_ref[qi:qi + 1, :] = jnp.sum(a, axis=0, keepdims=True)


def _sb_fold(wq, kk, vv, col0, qpos, newest, c_ref, acc_ref):
    n = kk.shape[0]
    t2 = _suffix_matrix(LANES)
    s = _dot_nt(wq, kk)
    col = col0 + lax.broadcasted_iota(jnp.int32, (1, n), 1)
    c = c_ref[...]
    parts = []
    for hh in reversed(range(n // LANES)):
        z = s[:, hh * LANES:(hh + 1) * LANES]
        lk = -_softplus(z)
        if newest:
            prior = col[:, hh * LANES:(hh + 1) * LANES] < qpos
            lk = jnp.where(prior, lk, 0.0)
        later = c + jnp.dot(jnp.concatenate(_split_hi_lo(lk), axis=1), t2, preferred_element_type=F32)
        a = jnp.exp(z + lk + later)
        parts.append(jnp.where(prior, a, 0.0) if newest else a)
        c = c + jnp.sum(lk, axis=-1, keepdims=True)
    a_all = parts[0] if len(parts) == 1 else jnp.concatenate(parts[::-1], axis=1)
    acc_ref[...] = acc_ref[...] + jnp.dot(a_all.astype(BF16), vv, preferred_element_type=F32)
    c_ref[...] = c


SB_CHUNK_PAGES = 2


def _sb_sample_kernel(pt_ref, qrep_ref, knew_ref, vnew_ref, ck_ref, cv_ref, o_ref,
                      wq_sc, c_sc, acc_sc, kbuf, vbuf, sem, *, layer, n_pages, past, n_new):
    b = pl.program_id(0)
    rows, d = acc_sc.shape
    page = kbuf.shape[1] // SB_CHUNK_PAGES
    n_chunks = n_pages // SB_CHUNK_PAGES
    row_id = lax.broadcasted_iota(jnp.int32, (rows, 1), 0)
    qpos = past + row_id // REPL

    def copies(t, slot):
        out = []
        for u in range(SB_CHUNK_PAGES):
            pg = pt_ref[b * n_pages + SB_CHUNK_PAGES * t + u]
            dst = pl.ds(u * page, page)
            out.append(pltpu.make_async_copy(ck_ref.at[layer, pg], kbuf.at[slot, dst], sem.at[slot, u]))
            out.append(pltpu.make_async_copy(cv_ref.at[layer, pg], vbuf.at[slot, dst],
                                             sem.at[slot, SB_CHUNK_PAGES + u]))
        return out

    for cp in copies(n_chunks - 1, 0):
        cp.start()
    col = lax.broadcasted_iota(jnp.int32, (1, d), 1)
    wq = jnp.where(col // HEAD_DIM == _row_segment(row_id, KIND_SB), qrep_ref[...], 0.0)
    wq_sc[...] = (wq * QK_SCALE).astype(BF16)
    c_sc[...] = jnp.zeros(c_sc.shape, F32)
    acc_sc[...] = jnp.zeros(acc_sc.shape, F32)
    _sb_fold(wq_sc[...], knew_ref[...], vnew_ref[...], past, qpos, True, c_sc, acc_sc)

    def is_live():
        return (jnp.max(c_sc[...]) > SB_LOG_ZERO).astype(jnp.int32)

    def cond(carry):
        t, live = carry
        return jnp.logical_and(t >= 0, live > 0)

    def body(carry):
        t, _ = carry
        slot = (n_chunks - 1 - t) & 1
        for cp in copies(t, slot):
            cp.wait()

        @pl.when(t >= 1)
        def _():
            for cp in copies(t - 1, slot ^ 1):
                cp.start()

        _sb_fold(wq_sc[...], kbuf[slot].astype(BF16), vbuf[slot].astype(BF16),
                 t * SB_CHUNK_PAGES * page, qpos, False, c_sc, acc_sc)
        return t - 1, is_live()

    t_end, _ = lax.while_loop(cond, body, (n_chunks - 1, is_live()))

    @pl.when(t_end >= 0)
    def _():
        for cp in copies(t_end, (n_chunks - 1 - t_end) & 1):
            cp.wait()

    r = lax.broadcasted_iota(jnp.int32, (REPL, 1), 0)
    own_cols = col // HEAD_DIM == r
    for qi in range(n_new):
        a = jnp.where(own_cols, acc_sc[qi * REPL:(qi + 1) * REPL, :], 0.0)
        o_ref[qi:qi + 1, :] = jnp.sum(a, axis=0, keepdims=True)


def _sb_sample_mixer(layer, page_table, q, k_new, v_new, cache_k, cache_v):
    db, n_new, d = q.shape
    n_pages = page_table.shape[1]
    page = cache_k.shape[2]
    assert page == LANES and n_pages % SB_CHUNK_PAGES == 0 and n_new <= page and (n_new * REPL) % 16 == 0
    qrep = jnp.repeat(q, REPL, axis=1)
    pad = ((0, 0), (0, page - n_new), (0, 0))
    return _sb_sample_call(layer, db, n_new, d, n_pages, page)(
        page_table.reshape(-1), qrep, jnp.pad(k_new, pad).astype(BF16), jnp.pad(v_new, pad).astype(BF16),
        cache_k, cache_v)


def _sb_sample_call(layer, db, n_new, d, n_pages, page):
    rows = n_new * REPL

    def per_sample(shape):
        return pl.BlockSpec((None,) + shape, lambda b, pt: (b, 0, 0))

    hbm = pl.BlockSpec(memory_space=pl.ANY)
    chunk = SB_CHUNK_PAGES * page
    grid_spec = pltpu.PrefetchScalarGridSpec(
        num_scalar_prefetch=1,
        grid=(db,),
        in_specs=[per_sample((rows, d)), per_sample((page, d)), per_sample((page, d)), hbm, hbm],
        out_specs=per_sample((n_new, d)),
        scratch_shapes=[pltpu.VMEM((rows, d), BF16), pltpu.VMEM((rows, 1), F32), pltpu.VMEM((rows, d), F32),
                        pltpu.VMEM((2, chunk, d), F32), pltpu.VMEM((2, chunk, d), F32),
                        pltpu.SemaphoreType.DMA((2, 2 * SB_CHUNK_PAGES))],
    )
    return pl.pallas_call(
        functools.partial(_sb_sample_kernel, layer=layer, n_pages=n_pages, past=n_pages * page, n_new=n_new),
        grid_spec=grid_spec,
        out_shape=jax.ShapeDtypeStruct((db, n_new, d), F32),
        compiler_params=pltpu.CompilerParams(vmem_limit_bytes=VMEM_LIMIT, dimension_semantics=("arbitrary",)),
        name="sb_sample",
    )


def _sample_stream_mixer(kind, layer, page_table, q, k_new, v_new, cache_k, cache_v, lam_vecs, sub_gain):
    db, n_new, d = q.shape
    n_pages = page_table.shape[1]
    page = cache_k.shape[2]
    past = n_pages * page
    pages = 8 if n_pages % 8 == 0 else 2
    rows = n_new * REPL
    assert page == LANES and n_pages % pages == 0 and n_new <= page and rows % 16 == 0
    lam_init = 0.8 - 0.6 * math.exp(-0.3 * layer)

    qrep = jnp.repeat(q, REPL, axis=1)
    pad = ((0, 0), (0, page - n_new), (0, 0))
    knew = jnp.pad(k_new, pad).astype(BF16)
    vnew = jnp.pad(v_new, pad).astype(BF16)
    slopes = _alibi_slopes(DIFF_HEADS)[(np.arange(rows) % REPL) % DIFF_HEADS]
    slope_rows = jnp.asarray(slopes.reshape(rows, 1))
    gain_full = jnp.tile(sub_gain, (1, d // LANES))

    def per_sample(shape):
        return pl.BlockSpec((None,) + shape, lambda b, s, pt: (b, 0, 0))

    def const(shape):
        return pl.BlockSpec(shape, lambda b, s, pt: (0, 0))

    def page_spec(p):
        return pl.BlockSpec((None, None, page, d),
                            lambda b, s, pt: (layer, pt[b * n_pages + n_pages - 1 - s * pages - p], 0, 0))

    vec = const((1, HEAD_DIM))
    grid_spec = pltpu.PrefetchScalarGridSpec(
        num_scalar_prefetch=1,
        grid=(db, n_pages // pages),
        in_specs=[per_sample((rows, d)), per_sample((page, d)), per_sample((page, d)),
                  const((rows, 1)), vec, vec, vec, vec, const((1, d))]
                 + [page_spec(p) for p in range(pages)] * 2,
        out_specs=per_sample((n_new, d)),
        scratch_shapes=[pltpu.VMEM((rows, d), BF16), pltpu.VMEM((rows, 1), F32), pltpu.VMEM((rows, 1), F32),
                        pltpu.VMEM((rows, d), F32), pltpu.SMEM((1,), jnp.int32)],
    )
    return pl.pallas_call(
        functools.partial(_sample_stream_kernel, kind=kind, pages=pages, n_pages=n_pages, past=past,
                          n_new=n_new, lam_init=lam_init),
        grid_spec=grid_spec,
        out_shape=jax.ShapeDtypeStruct((db, n_new, d), F32),
        compiler_params=pltpu.CompilerParams(vmem_limit_bytes=VMEM_LIMIT,
                                             dimension_semantics=("arbitrary", "arbitrary")),
        name=("diff_sample", "sb_sample")[kind],
    )(page_table.reshape(-1), qrep, knew, vnew, slope_rows, *lam_vecs, gain_full,
      *([cache_k] * pages), *([cache_v] * pages))


def _sample_mixer(kind, layer, page_table, q, k_new, v_new, cache_k, cache_v, lam_vecs, sub_gain):
    if kind == KIND_SB:
        return _sb_sample_mixer(layer, page_table, q, k_new, v_new, cache_k, cache_v)
    if kind == KIND_DIFF:
        return _sample_stream_mixer(kind, layer, page_table, q, k_new, v_new, cache_k, cache_v,
                                    lam_vecs, sub_gain)
    db, n_new, d = q.shape
    n_pages = page_table.shape[1]
    page = cache_k.shape[2]
    past = n_pages * page
    pages = 8 if n_pages % 8 == 0 else 2
    nkc = n_pages // pages
    rows = n_new * REPL
    width = past + page
    assert MOBA_BLOCK == 2 * page and past % MOBA_BLOCK == 0 and past // MOBA_BLOCK < LANES
    assert n_pages % pages == 0 and n_new <= page and rows % 16 == 0
    lam_init = 0.8 - 0.6 * math.exp(-0.3 * layer)

    qrep = jnp.repeat(q, REPL, axis=1)
    pad = ((0, 0), (0, page - n_new), (0, 0))
    knew = jnp.pad(k_new, pad).astype(BF16)
    vnew = jnp.pad(v_new, pad).astype(BF16)
    rep = np.arange(rows) % REPL
    if kind == KIND_DIFF:
        slopes = _alibi_slopes(DIFF_HEADS)[rep % DIFF_HEADS]
    elif kind == KIND_MOBA:
        slopes = _alibi_slopes(MOBA_HEADS)[rep]
    else:
        slopes = np.zeros((rows,), np.float32)
    slope_rows = jnp.asarray(slopes.reshape(rows, 1))
    gain_full = jnp.tile(sub_gain, (1, d // LANES))

    def per_sample(shape):
        return pl.BlockSpec((None,) + shape, lambda b, s, pt: (b, 0, 0))

    def const(shape):
        return pl.BlockSpec(shape, lambda b, s, pt: (0, 0))

    def k_spec(p):
        return pl.BlockSpec(
            (None, None, page, d),
            lambda b, s, pt: (layer, pt[b * n_pages + jnp.minimum(s, nkc - 1) * pages + p], 0, 0))

    def v_spec(p):
        return pl.BlockSpec(
            (None, None, page, d),
            lambda b, s, pt: (layer, pt[b * n_pages + jnp.maximum(s - nkc, 0) * pages + p], 0, 0))

    vec = const((1, HEAD_DIM))
    grid_spec = pltpu.PrefetchScalarGridSpec(
        num_scalar_prefetch=1,
        grid=(db, 2 * nkc),
        in_specs=[per_sample((rows, d)), per_sample((page, d)), per_sample((page, d)),
                  const((rows, 1)), vec, vec, vec, vec, const((1, d))]
                 + [k_spec(p) for p in range(pages)] + [v_spec(p) for p in range(pages)],
        out_specs=per_sample((n_new, d)),
        scratch_shapes=[pltpu.VMEM((rows, d), BF16), pltpu.VMEM((rows, d), F32),
                        pltpu.VMEM((rows, width), F32), pltpu.VMEM((rows, width), F32),
                        pltpu.VMEM((rows, d), F32), pltpu.VMEM((rows, LANES), F32)],
    )
    return pl.pallas_call(
        functools.partial(_sample_kernel, kind=kind, pages=pages, nkc=nkc, past=past,
                          n_new=n_new, lam_init=lam_init),
        grid_spec=grid_spec,
        out_shape=jax.ShapeDtypeStruct((db, n_new, d), F32),
        compiler_params=pltpu.CompilerParams(vmem_limit_bytes=VMEM_LIMIT),
        name=("diff_sample", "sb_sample", "moba_sample")[kind],
    )(page_table.reshape(-1), qrep, knew, vnew, slope_rows, *lam_vecs, gain_full,
      *([cache_k] * pages), *([cache_v] * pages))


def kernel(x_prompt, x_sample, cache_k, cache_v, page_table, w_in, w_out, ln_gain, ln_bias,
           diff_lambda_q1, diff_lambda_k1, diff_lambda_q2, diff_lambda_k2, diff_subln_gain):
    depth = w_in.shape[0]
    b, s, d = x_prompt.shape
    db, ds, _ = x_sample.shape
    alpha = (2 * depth) ** 0.25
    tm_in = min(512, b * s)
    tm_out = min(512, b * s)
    xp = x_prompt.reshape(b * s, d)
    xs = x_sample.reshape(db * ds, d)
    k_all = jnp.zeros((depth, b * s, d), F32)
    v_all = jnp.zeros((depth, b * s, d), F32)
    ks_rows, vs_rows = [], []
    for i in range(depth):
        kind = i % N_MIXERS
        j = i // N_MIXERS
        lam_vecs = [v[j].reshape(1, HEAD_DIM) for v in
                    (diff_lambda_q1, diff_lambda_k1, diff_lambda_q2, diff_lambda_k2)]
        sub_gain = diff_subln_gain[j].reshape(1, LANES)
        w_in_b = w_in[i].astype(BF16)
        w_out_b = w_out[i].astype(BF16)

        qt, k_all, kb, v_all, vt, g = _in_proj_prompt(xp, w_in_b[:, 0:d].T, w_in_b[:, 2 * d:3 * d].T,
                                                      w_in_b[:, d:4 * d], tm_in, i, depth, k_all, v_all)
        o = _prompt_mixer(kind, i, b, s, qt, k_all.reshape(depth, b, s, d), kb.reshape(b, s, d), vt,
                          lam_vecs, sub_gain)
        xp = _out_proj(o.reshape(b * s, d), g, xp, w_out_b, ln_gain[i], ln_bias[i], alpha, tm_out)

        qs, ks, vs, gs = _in_proj_rows(xs, w_in_b, db * ds)
        sshp = (db, ds, d)
        os_ = _sample_mixer(kind, i, page_table, qs.reshape(sshp), ks.reshape(sshp), vs.reshape(sshp),
                            cache_k, cache_v, lam_vecs, sub_gain)
        xs = _out_proj(os_.reshape(db * ds, d), gs, xs, w_out_b, ln_gain[i], ln_bias[i], alpha, db * ds)
        ks_rows.append(ks.reshape(sshp))
        vs_rows.append(vs.reshape(sshp))
    return (xp.reshape(b, s, d), xs.reshape(db, ds, d), k_all.reshape(depth, b, s, d),
            v_all.reshape(depth, b, s, d), jnp.stack(ks_rows), jnp.stack(vs_rows))
```

```python
import functools
import math

import ml_dtypes
import numpy as np
import jax
import jax.numpy as jnp
from jax import lax
from jax.experimental import pallas as pl
from jax.experimental.pallas import tpu as pltpu

F32 = jnp.float32
BF16 = jnp.bfloat16

N_MIXERS = 3
DIFF_HEADS = 8
HEAD_DIM = 64
SB_HEADS = 16
MOBA_HEADS = 16
MOBA_BLOCK = 256
MOBA_TOPK = 3
LN_EPS = 1e-5
QK_SCALE = HEAD_DIM ** -0.5
LOG2E = 1.4426950408889634
LN2 = 0.6931471805599453

LANES = 128
BF16_ROWS = 16
V_ROWS = LANES + BF16_ROWS
VMEM_LIMIT = 56 * 1024 * 1024

SB_LOG_ZERO = -104.5
ALIBI_LOG_ZERO = -106.0

KIND_DIFF, KIND_SB, KIND_MOBA = 0, 1, 2
NEG_INF = float("-inf")


def _alibi_slopes(n_heads):
    return np.asarray(2.0 ** (-8.0 * np.arange(1, n_heads + 1) / n_heads), dtype=np.float32)


def _slope_table(n_heads):
    s = _alibi_slopes(n_heads)
    rest = (s.astype(np.float64) * LOG2E).astype(np.float32)
    parts = []
    for _ in range(3):
        p = rest.astype(ml_dtypes.bfloat16).astype(np.float32)
        parts.append(p)
        rest = rest - p
    return jnp.asarray(np.stack([s, (1.0 / s).astype(np.float32)] + parts))


def _alibi_key_codes(tk):
    r = lax.broadcasted_iota(jnp.int32, (tk, LANES), 0)
    lane = lax.broadcasted_iota(jnp.int32, (tk, LANES), 1)
    code = jnp.where(lane < 3, r - (r & 1), jnp.where(lane < 6, r & 1, 0))
    return code.astype(F32).astype(BF16)


def _alibi_query_rows(parts, tq):
    row = lax.broadcasted_iota(jnp.int32, (LANES, tq), 0)
    val = jnp.where(row % 3 == 0, parts[0], jnp.where(row % 3 == 1, parts[1], parts[2]))
    return jnp.where(row < 6, val, 0.0).astype(BF16)


def _dot_nt(a, b):
    return lax.dot_general(a, b, (((1,), (1,)), ((), ())), preferred_element_type=F32)


def _diff_lambda(lq1, lk1, lq2, lk2, lam_init):
    a = jnp.exp(jnp.sum(lq1[...] * lk1[...], axis=-1, keepdims=True))
    b = jnp.exp(jnp.sum(lq2[...] * lk2[...], axis=-1, keepdims=True))
    return a - b + lam_init


def _top_select(gate, valid, axis):
    pos = lax.broadcasted_iota(jnp.int32, gate.shape, axis).astype(F32)
    g = jnp.where(valid, gate, NEG_INF)
    sel = jnp.zeros(gate.shape, F32)
    for _ in range(MOBA_TOPK):
        m = jnp.max(g, axis=axis, keepdims=True)
        idx = jnp.min(jnp.where(g == m, pos, 1e9), axis=axis, keepdims=True)
        idx = jnp.where(m > NEG_INF, idx, 1e9)
        pick = pos == idx
        sel = jnp.where(pick, 1.0, sel)
        g = jnp.where(pick, NEG_INF, g)
    return sel


def _softplus(z):
    return jnp.maximum(z, 0.0) + jnp.log1p(jnp.exp(-jnp.abs(z)))


def _split_hi_lo(x):
    hi = x.astype(BF16)
    return hi, (x - hi.astype(F32)).astype(BF16)


def _in_proj_rows_kernel(x_ref, w_ref, q_ref, k_ref, v_ref, g_ref):
    xb = x_ref[...].astype(BF16)
    d = q_ref.shape[-1]
    for j, o_ref in enumerate((q_ref, k_ref, v_ref, g_ref)):
        o_ref[...] = jnp.dot(xb, w_ref[:, j * d:(j + 1) * d], preferred_element_type=F32)


def _in_proj_rows(x2d, w_bf16, tm):
    n, d = x2d.shape
    row = pl.BlockSpec((tm, d), lambda i: (i, 0))
    return pl.pallas_call(
        _in_proj_rows_kernel,
        grid=(n // tm,),
        in_specs=[row, pl.BlockSpec((d, 4 * d), lambda i: (0, 0))],
        out_specs=[row] * 4,
        out_shape=[jax.ShapeDtypeStruct((n, d), F32)] * 4,
        compiler_params=pltpu.CompilerParams(vmem_limit_bytes=VMEM_LIMIT),
        name="in_proj_rows",
    )(x2d, w_bf16)


def _in_proj_prompt_kernel(x_ref, wqt_ref, wvt_ref, w_ref, k_all_ref, v_all_ref,
                           qt_ref, k_ref, kb_ref, v_ref, vt_ref, g_ref):
    del k_all_ref, v_all_ref
    xb = x_ref[...].astype(BF16)
    d = k_ref.shape[-1]
    qt = _dot_nt(wqt_ref[...], xb)
    vt = _dot_nt(wvt_ref[...], xb).astype(BF16)
    ones = jnp.ones((V_ROWS - LANES, vt.shape[1]), BF16)
    for gi in range(d // LANES):
        qt_ref[gi] = qt[gi * LANES:(gi + 1) * LANES, :]
        vt_ref[gi, 0:LANES, :] = vt[gi * LANES:(gi + 1) * LANES, :]
        vt_ref[gi, LANES:V_ROWS, :] = ones
    k = jnp.dot(xb, w_ref[:, 0:d], preferred_element_type=F32)
    k_ref[...] = k
    kb_ref[...] = k.astype(BF16)
    v_ref[...] = jnp.dot(xb, w_ref[:, d:2 * d], preferred_element_type=F32)
    g_ref[...] = jnp.dot(xb, w_ref[:, 2 * d:3 * d], preferred_element_type=F32)


def _in_proj_prompt(x2d, wqt, wvt, w_kvg, tm, layer, depth, k_all, v_all):
    n, d = x2d.shape
    groups = d // LANES
    row = pl.BlockSpec((tm, d), lambda i: (i, 0))
    sq = pl.BlockSpec((d, d), lambda i: (0, 0))
    layer_row = pl.BlockSpec((None, tm, d), lambda i: (layer, i, 0))
    stack = jax.ShapeDtypeStruct((depth, n, d), F32)
    hbm = pl.BlockSpec(memory_space=pl.ANY)
    return pl.pallas_call(
        _in_proj_prompt_kernel,
        grid=(n // tm,),
        in_specs=[row, sq, sq, pl.BlockSpec((d, 3 * d), lambda i: (0, 0)), hbm, hbm],
        out_specs=[pl.BlockSpec((groups, LANES, tm), lambda i: (0, 0, i)), layer_row, row, layer_row,
                   pl.BlockSpec((groups, V_ROWS, tm), lambda i: (0, 0, i)), row],
        out_shape=[jax.ShapeDtypeStruct((groups, LANES, n), F32), stack, jax.ShapeDtypeStruct((n, d), BF16),
                   stack, jax.ShapeDtypeStruct((groups, V_ROWS, n), BF16), jax.ShapeDtypeStruct((n, d), F32)],
        input_output_aliases={4: 1, 5: 3},
        compiler_params=pltpu.CompilerParams(vmem_limit_bytes=VMEM_LIMIT),
        name="in_proj_prompt",
    )(x2d, wqt, wvt, w_kvg, k_all, v_all)


def _out_proj_kernel(o_ref, g_ref, x_ref, w_ref, gain_ref, bias_ref, y_ref, *, alpha):
    gt = g_ref[...]
    og = o_ref[...] * (gt / (1.0 + jnp.exp(-gt)))
    y = jnp.dot(og.astype(BF16), w_ref[...], preferred_element_type=F32)
    z = alpha * x_ref[...] + y
    mu = jnp.mean(z, axis=-1, keepdims=True)
    zc = z - mu
    var = jnp.mean(zc * zc, axis=-1, keepdims=True)
    y_ref[...] = zc * lax.rsqrt(var + LN_EPS) * gain_ref[...] + bias_ref[...]


def _out_proj(o2d, g2d, x2d, w_bf16, gain, bias, alpha, tm):
    n, d = x2d.shape
    row = pl.BlockSpec((tm, d), lambda i: (i, 0))
    vec = pl.BlockSpec((1, d), lambda i: (0, 0))
    return pl.pallas_call(
        functools.partial(_out_proj_kernel, alpha=alpha),
        grid=(n // tm,),
        in_specs=[row, row, row, pl.BlockSpec((d, d), lambda i: (0, 0)), vec, vec],
        out_specs=row,
        out_shape=jax.ShapeDtypeStruct((n, d), F32),
        compiler_params=pltpu.CompilerParams(vmem_limit_bytes=VMEM_LIMIT),
        name="out_proj_ln",
    )(o2d, g2d, x2d, w_bf16, gain.reshape(1, d), bias.reshape(1, d))


def _split_maps_t(qt):
    row = lax.broadcasted_iota(jnp.int32, qt.shape, 0)
    return [jnp.where(row < HEAD_DIM, qt, 0.0), jnp.where(row >= HEAD_DIM, qt, 0.0)]


def _key_abs_max(k_ref, seq, chunk):
    def body(it, acc):
        kc = k_ref[pl.ds(pl.multiple_of(it * chunk, chunk), chunk), :].astype(F32)
        return jnp.maximum(acc, jnp.max(jnp.abs(kc), axis=0, keepdims=True))

    kmax = lax.fori_loop(0, seq // chunk, body, jnp.zeros((1, LANES), F32))
    return jnp.broadcast_to(kmax, (BF16_ROWS, LANES)).astype(BF16)


def _score_bound(kabs, qm_f32):
    return jnp.dot(kabs, jnp.abs(qm_f32).astype(BF16), preferred_element_type=F32)[0:1]


def _first_live_tile(excess, slope_inv, q0, tk, hi):
    e = jnp.max(excess, axis=-1, keepdims=True) * LN2
    lim = ((ALIBI_LOG_ZERO - e) * slope_inv + (q0 - tk + 1).astype(F32)) * (1.0 / tk)
    j_lo = jnp.clip(jnp.ceil(lim), 0.0, hi.astype(F32)).astype(jnp.int32)
    return jnp.max(j_lo)


def _softmax_tile(u_ref, m_tile, shift, m_prev, vt, acc_ref):
    m_new = jnp.maximum(m_prev, m_tile + shift)
    m_fin = jnp.where(m_new > NEG_INF, m_new, 0.0)
    p = jnp.exp2(u_ref[...] - (m_fin - shift)).astype(BF16)
    pv = jnp.dot(vt, p, preferred_element_type=F32)
    acc_ref[...] = jnp.exp2(m_prev - m_fin) * acc_ref[...] + pv
    return m_new


def _pipelined_sweep(n_diag, first_diag, scores, update, first_live, m_init):
    m = m_init
    mt_cur = scores(first_diag, 0, 0)
    j_lo = None
    last_past = jnp.maximum(first_diag - 1, 0)
    for n in range(n_diag):
        if n + 1 < n_diag:
            mt_nxt = scores(first_diag + n + 1, (n + 1) & 1, n + 1)
        else:
            j_lo = first_live(m, mt_cur, n)
            mt_nxt = scores(jnp.minimum(j_lo, last_past), n_diag & 1, None)
        m = update(first_diag + n, n & 1, mt_cur, m)
        mt_cur = mt_nxt

    slot_a = n_diag & 1
    slot_b = slot_a ^ 1
    n_past = first_diag - j_lo

    def body(pair, carry):
        m_prev, mt_a = carry
        a = j_lo + 2 * pair
        mt_b = scores(a + 1, slot_b, None)
        m_mid = update(a, slot_a, mt_a, m_prev)
        mt_next = scores(jnp.minimum(a + 2, last_past), slot_a, None)
        return update(a + 1, slot_b, mt_b, m_mid), mt_next

    m, mt_cur = lax.fori_loop(0, n_past // 2, body, (m, mt_cur))

    @pl.when(n_past % 2 == 1)
    def _():
        update(last_past, slot_a, mt_cur, m)


def _diff_prompt_kernel(slope_ref, qt_ref, k_ref, vt_ref, lq1, lk1, lq2, lk2, gain_ref, o_ref,
                        kcode_sc, kabs_sc, acc_sc, u_sc, *, t, seq, lam_init):
    g = pl.program_id(1)
    i = pl.program_id(2)
    slope2 = slope_ref[0, g] * LOG2E

    @pl.when(i == 0)
    def _():
        kcode_sc[...] = _alibi_key_codes(t)
        kabs_sc[...] = _key_abs_max(k_ref, seq, t)

    qf = _split_maps_t(qt_ref[...] * (QK_SCALE * LOG2E))
    slope_rows = _alibi_query_rows([slope_ref[2 + e, g] for e in range(3)], t)
    qm = [jnp.concatenate([x.astype(BF16), slope_rows], axis=0) for x in qf]
    q0 = i * t
    acc_sc[...] = jnp.zeros(acc_sc.shape, F32)

    def scores(j, slot, diag_n):
        kt = jnp.concatenate([k_ref[pl.ds(pl.multiple_of(j * t, t), t), :], kcode_sc[...]], axis=1)
        maxima = []
        for c in range(2):
            u = jnp.dot(kt, qm[c], preferred_element_type=F32)
            if diag_n is not None:
                krow = lax.broadcasted_iota(jnp.int32, (t, t), 0)
                qcol = lax.broadcasted_iota(jnp.int32, (t, t), 1)
                u = jnp.where(krow <= qcol, u, NEG_INF)
            u_sc[slot, c] = u
            maxima.append(jnp.max(u, axis=0, keepdims=True))
        return tuple(maxima)

    def update(j, slot, maxima, m_prev):
        k0 = pl.multiple_of(j * t, t)
        vt = vt_ref[:, pl.ds(k0, t)]
        shift = slope2 * (jnp.zeros((1, t), jnp.int32) + (k0 - q0)).astype(F32)
        return tuple(_softmax_tile(u_sc.at[slot, c], maxima[c], shift, m_prev[c], vt, acc_sc.at[c])
                     for c in range(2))

    def first_live(m_prev, maxima, n):
        del m_prev, n
        kabs = kabs_sc[...]
        excess = jnp.maximum(_score_bound(kabs, qf[0]) - maxima[0], _score_bound(kabs, qf[1]) - maxima[1])
        return _first_live_tile(excess, slope_ref[1, g], q0, t, i)

    neg = jnp.full((1, t), NEG_INF, F32)
    _pipelined_sweep(1, i, scores, update, first_live, (neg, neg))

    lam = _diff_lambda(lq1, lk1, lq2, lk2, lam_init)
    o = (acc_sc[0, 0:LANES, :] / acc_sc[0, LANES:LANES + 1, :]
         - lam * (acc_sc[1, 0:LANES, :] / acc_sc[1, LANES:LANES + 1, :]))
    ms = jnp.mean(o * o, axis=0, keepdims=True)
    o_ref[...] = (o * lax.rsqrt(ms + LN_EPS)).T * gain_ref[...] * (1.0 - lam_init)


def _moba_prompt_kernel(slope_ref, qt_ref, k_ref, vt_ref, mean_ref, o_ref,
                        kcode_sc, kabs_sc, acc_sc, sel_sc, u_sc, *, tq, seq):
    tk = MOBA_BLOCK
    n_diag = tq // tk
    g = pl.program_id(1)
    i = pl.program_id(2)
    slopes = [slope_ref[0, 2 * g] * LOG2E, slope_ref[0, 2 * g + 1] * LOG2E]

    @pl.when(i == 0)
    def _():
        kcode_sc[...] = _alibi_key_codes(tk)
        kabs_sc[...] = _key_abs_max(k_ref, seq, tk)

    qf = _split_maps_t(qt_ref[...])
    qs = [x * (QK_SCALE * LOG2E) for x in qf]
    qm = [jnp.concatenate([qs[h].astype(BF16),
                           _alibi_query_rows([slope_ref[2 + e, 2 * g + h] for e in range(3)], tq)], axis=0)
          for h in range(2)]
    means = mean_ref[...]
    first_diag = i * n_diag
    col_blk = lax.broadcasted_iota(jnp.int32, (1, tq), 1) // tk
    blk = lax.broadcasted_iota(jnp.int32, (means.shape[0], tq), 0)
    for h in range(2):
        gate = jnp.dot(means, qf[h], precision=lax.Precision.HIGHEST, preferred_element_type=F32)
        sel_sc[h] = _top_select(gate, blk < first_diag + col_blk, axis=0)
    q0 = i * tq
    acc_sc[...] = jnp.zeros(acc_sc.shape, F32)

    def scores(j, slot, diag_n):
        kt = jnp.concatenate([k_ref[pl.ds(pl.multiple_of(j * tk, tk), tk), :], kcode_sc[...]], axis=1)
        maxima = []
        for h in range(2):
            u = jnp.dot(kt, qm[h], preferred_element_type=F32)
            picked = sel_sc[h, pl.ds(j, 1), :]
            if diag_n is None:
                u = jnp.where(picked > 0.0, u, NEG_INF)
            else:
                allowed = jnp.where(col_blk == diag_n, 1.0, picked)
                krow = lax.broadcasted_iota(jnp.int32, (tk, tq), 0) + diag_n * tk
                qcol = lax.broadcasted_iota(jnp.int32, (tk, tq), 1)
                u = jnp.where(krow <= qcol, jnp.where(allowed > 0.0, u, NEG_INF), NEG_INF)
            u_sc[slot, h] = u
            maxima.append(jnp.max(u, axis=0, keepdims=True))
        return tuple(maxima)

    def update(j, slot, maxima, m_prev):
        k0 = pl.multiple_of(j * tk, tk)
        vt = vt_ref[:, pl.ds(k0, tk)]
        rel = (jnp.zeros((1, tq), jnp.int32) + (k0 - q0)).astype(F32)
        return tuple(_softmax_tile(u_sc.at[slot, h], maxima[h], slopes[h] * rel, m_prev[h], vt, acc_sc.at[h])
                     for h in range(2))

    def first_live(m_prev, maxima, n):
        kabs = kabs_sc[...]
        j_lo = None
        for h in range(2):
            m_h = jnp.maximum(m_prev[h], maxima[h] + slopes[h] * float(n * tk))
            j_h = _first_live_tile(_score_bound(kabs, qs[h]) - m_h, slope_ref[1, 2 * g + h], q0, tk, first_diag)
            j_lo = j_h if j_lo is None else jnp.minimum(j_lo, j_h)
        return j_lo

    neg = jnp.full((1, tq), NEG_INF, F32)
    _pipelined_sweep(n_diag, first_diag, scores, update, first_live, (neg, neg))

    t = tq
    row = lax.broadcasted_iota(jnp.int32, (LANES, t), 0)
    o = jnp.where(row < HEAD_DIM, acc_sc[0, 0:LANES, :] / acc_sc[0, LANES:LANES + 1, :],
                  acc_sc[1, 0:LANES, :] / acc_sc[1, LANES:LANES + 1, :])
    o_ref[...] = o.T


def _suffix_matrix_t(tk):
    s = lax.broadcasted_iota(jnp.int32, (tk + BF16_ROWS, 2 * tk), 0)
    j = lax.broadcasted_iota(jnp.int32, (tk + BF16_ROWS, 2 * tk), 1)
    return jnp.where(s >= tk, 1.0, jnp.where((j & (tk - 1)) > s, 1.0, 0.0)).astype(BF16)


def _sb_prompt_kernel(*refs, batch, seq, tq, tk):
    qt_refs = refs[:batch]
    k_ref, vt_ref, o_ref, acc_sc = refs[batch:]
    i = pl.program_id(1)
    chains = [(b, h) for b in range(batch) for h in range(2)]
    qm = {}
    for b in range(batch):
        for h, x in enumerate(_split_maps_t(qt_refs[b][...] * QK_SCALE)):
            qm[b, h] = x.astype(BF16)
    acc_sc[...] = jnp.zeros(acc_sc.shape, F32)
    qpos = i * tq + lax.broadcasted_iota(jnp.int32, (tk, tq), 1)
    krow = lax.broadcasted_iota(jnp.int32, (tk, tq), 0)
    tmat = _suffix_matrix_t(tk)

    def chunk(j, c_prev):
        k0 = pl.multiple_of(j * tk, tk)
        kt = [k_ref[b, pl.ds(k0, tk), :] for b in range(batch)]
        vt = [vt_ref[0:LANES, pl.ds(pl.multiple_of(b * seq + k0, tk), tk)] for b in range(batch)]
        past = (k0 + krow) < qpos
        z = [jnp.dot(kt[b], qm[b, h], preferred_element_type=F32) for b, h in chains]
        lk = [jnp.where(past, -_softplus(zz), 0.0) for zz in z]
        suf = [jnp.dot(tmat, jnp.concatenate(_split_hi_lo(x), axis=0), preferred_element_type=F32) for x in lk]
        c_new = []
        for n, (b, h) in enumerate(chains):
            later = c_prev[n] + suf[n][0:tk]
            a = jnp.where(past, jnp.exp(z[n] + lk[n] + later), 0.0)
            acc_sc[b, h] = acc_sc[b, h] + jnp.dot(vt[b], a.astype(BF16), preferred_element_type=F32)
            c_new.append(c_prev[n] + suf[n][tk:tk + 1])
        return tuple(c_new)

    def cond(carry):
        return jnp.logical_and(carry[0] >= 0, carry[1] > 0)

    def body(carry):
        c = chunk(carry[0], carry[2:])
        c_max = c[0]
        for x in c[1:]:
            c_max = jnp.maximum(c_max, x)
        live = jnp.max(c_max) > SB_LOG_ZERO
        return (carry[0] - 1, live.astype(jnp.int32)) + c

    zero = jnp.zeros((1, tq), F32)
    lax.while_loop(cond, body, (i * (tq // tk) + tq // tk - 1, jnp.int32(1)) + (zero,) * len(chains))
    row = lax.broadcasted_iota(jnp.int32, (LANES, tq), 0)
    for b in range(batch):
        o_ref[b] = jnp.where(row < HEAD_DIM, acc_sc[b, 0], acc_sc[b, 1]).T


def _block_mean_kernel(k_ref, o_ref, *, per_step):
    k = k_ref[...]
    o_ref[...] = jnp.mean(k.reshape(per_step, MOBA_BLOCK, k.shape[-1]), axis=1)


def _block_means(k_all, layer):
    _, b, s, d = k_all.shape
    nb = s // MOBA_BLOCK
    per_step = 8 if nb % 8 == 0 else nb
    return pl.pallas_call(
        functools.partial(_block_mean_kernel, per_step=per_step),
        grid=(b, nb // per_step),
        in_specs=[pl.BlockSpec((None, None, per_step * MOBA_BLOCK, d), lambda bi, n: (layer, bi, n, 0))],
        out_specs=pl.BlockSpec((None, per_step, d), lambda bi, n: (bi, n, 0)),
        out_shape=jax.ShapeDtypeStruct((b, nb, d), F32),
        compiler_params=pltpu.CompilerParams(vmem_limit_bytes=VMEM_LIMIT),
        name="moba_block_means",
    )(k_all)


def _prompt_mixer(kind, layer, b, s, qt, k, kb, vt, lam_vecs, sub_gain):
    d = k.shape[-1]
    groups = d // LANES
    smem = pl.BlockSpec(memory_space=pltpu.SMEM)
    k_spec = pl.BlockSpec((None, s, LANES), lambda bi, g, i: (bi, 0, g))
    vt_spec = pl.BlockSpec((None, V_ROWS, s), lambda bi, g, i: (g, 0, bi))

    def q_spec(t):
        return pl.BlockSpec((None, LANES, t), lambda bi, g, i: (g, 0, bi * (s // t) + i))

    def o_spec(t):
        return pl.BlockSpec((None, t, LANES), lambda bi, g, i: (bi, i, g))

    params = pltpu.CompilerParams(vmem_limit_bytes=VMEM_LIMIT,
                                  dimension_semantics=("arbitrary", "arbitrary", "arbitrary"))
    out_shape = jax.ShapeDtypeStruct((b, s, d), F32)
    if kind == KIND_DIFF:
        t = min(512, s)
        lam_init = 0.8 - 0.6 * math.exp(-0.3 * layer)
        vec = pl.BlockSpec((1, HEAD_DIM), lambda bi, g, i: (0, 0))
        return pl.pallas_call(
            functools.partial(_diff_prompt_kernel, t=t, seq=s, lam_init=lam_init),
            grid=(b, groups, s // t),
            in_specs=[smem, q_spec(t), k_spec, vt_spec, vec, vec, vec, vec,
                      pl.BlockSpec((1, LANES), lambda bi, g, i: (0, 0))],
            out_specs=o_spec(t),
            out_shape=out_shape,
            scratch_shapes=[pltpu.VMEM((t, LANES), BF16), pltpu.VMEM((BF16_ROWS, LANES), BF16),
                            pltpu.VMEM((2, V_ROWS, t), F32), pltpu.VMEM((2, 2, t, t), F32)],
            compiler_params=params,
            name="diff_prompt",
        )(_slope_table(DIFF_HEADS), qt, kb, vt, *lam_vecs, sub_gain)
    if kind == KIND_SB:
        tq, tk = min(256, s), 128
        nq = s // tq
        return pl.pallas_call(
            functools.partial(_sb_prompt_kernel, batch=b, seq=s, tq=tq, tk=tk),
            grid=(groups, nq),
            in_specs=[pl.BlockSpec((None, LANES, tq), lambda g, i, bb=bb: (g, 0, bb * nq + i)) for bb in range(b)]
                     + [pl.BlockSpec((b, s, LANES), lambda g, i: (0, 0, g)),
                        pl.BlockSpec((None, V_ROWS, b * s), lambda g, i: (g, 0, 0))],
            out_specs=pl.BlockSpec((b, tq, LANES), lambda g, i: (0, i, g)),
            out_shape=out_shape,
            scratch_shapes=[pltpu.VMEM((b, 2, LANES, tq), F32)],
            compiler_params=pltpu.CompilerParams(vmem_limit_bytes=VMEM_LIMIT,
                                                 dimension_semantics=("arbitrary", "arbitrary")),
            name="sb_prompt",
        )(*([qt] * b), kb, vt)
    tq, tk = min(2 * MOBA_BLOCK, s), MOBA_BLOCK
    nb = s // tk
    means = _block_means(k, layer)
    return pl.pallas_call(
        functools.partial(_moba_prompt_kernel, tq=tq, seq=s),
        grid=(b, groups, s // tq),
        in_specs=[smem, q_spec(tq), k_spec, vt_spec,
                  pl.BlockSpec((None, nb, LANES), lambda bi, g, i: (bi, 0, g))],
        out_specs=o_spec(tq),
        out_shape=out_shape,
        scratch_shapes=[pltpu.VMEM((tk, LANES), BF16), pltpu.VMEM((BF16_ROWS, LANES), BF16),
                        pltpu.VMEM((2, V_ROWS, tq), F32), pltpu.VMEM((2, nb, tq), F32),
                        pltpu.VMEM((2, 2, tk, tq), F32)],
        compiler_params=params,
        name="moba_prompt",
    )(_slope_table(MOBA_HEADS), qt, kb, vt, means)


REPL = 16


def _row_segment(r, kind):
    rep = r % REPL
    if kind == KIND_DIFF:
        return 2 * (rep % DIFF_HEADS) + rep // DIFF_HEADS
    return rep


def _suffix_matrix(tk):
    j = lax.broadcasted_iota(jnp.int32, (2 * tk, tk), 0)
    s = lax.broadcasted_iota(jnp.int32, (2 * tk, tk), 1)
    return jnp.where((j & (tk - 1)) > s, 1.0, 0.0).astype(BF16)


def _sample_kernel(pt_ref, qrep_ref, knew_ref, vnew_ref, slope_ref, lq1, lk1, lq2, lk2, gain_ref,
                   *rest, kind, pages, nkc, past, n_new, lam_init):
    k_refs = rest[:pages]
    v_refs = rest[pages:2 * pages]
    o_ref = rest[2 * pages]
    wq_sc, wqf_sc, s_sc, p_sc, acc_sc, gate_sc = rest[2 * pages + 1:]
    del pt_ref
    step = pl.program_id(1)
    page = k_refs[0].shape[0]
    rows, d = wqf_sc.shape
    width = s_sc.shape[1]
    out_rows = rows // 2 if kind == KIND_DIFF else rows
    row_id = lax.broadcasted_iota(jnp.int32, (rows, 1), 0)
    qpos = past + row_id // REPL

    @pl.when(step == 0)
    def _():
        col = lax.broadcasted_iota(jnp.int32, (1, d), 1)
        wq = jnp.where(col // HEAD_DIM == _row_segment(row_id, kind), qrep_ref[...], 0.0)
        wqf_sc[...] = wq
        wq_sc[...] = (wq * QK_SCALE).astype(BF16)
        gate_sc[...] = jnp.zeros(gate_sc.shape, F32)

    @pl.when(step < nkc)
    def _():
        for pp in range(0, pages, 2):
            k0 = k_refs[pp][...]
            k1 = k_refs[pp + 1][...]
            kk = jnp.concatenate([k0.astype(BF16), k1.astype(BF16)], axis=0)
            col0 = pl.multiple_of((step * pages + pp) * page, 2 * page)
            s_sc[:, pl.ds(col0, 2 * page)] = _dot_nt(wq_sc[...], kk)
            if kind == KIND_MOBA:
                mean = (jnp.sum(k0, axis=0, keepdims=True) + jnp.sum(k1, axis=0, keepdims=True)) * (1.0 / MOBA_BLOCK)
                gcol = jnp.sum(wqf_sc[...] * mean, axis=-1, keepdims=True)
                lane = lax.broadcasted_iota(jnp.int32, gate_sc.shape, 1)
                gate_sc[...] = jnp.where(lane == (step * pages + pp) // 2, gcol, gate_sc[...])

    @pl.when(step == nkc)
    def _():
        s_sc[:, past:past + page] = _dot_nt(wq_sc[...], knew_ref[...])
        col = lax.broadcasted_iota(jnp.int32, (1, width), 1)
        if kind == KIND_SB:
            t2 = _suffix_matrix(page)
            n_chunks = width // page

            def body(it, c_prev):
                c0 = pl.multiple_of((n_chunks - 1 - it) * page, page)
                z = s_sc[:, pl.ds(c0, page)]
                colpos = c0 + lax.broadcasted_iota(jnp.int32, (1, page), 1)
                prior = colpos < qpos
                lk = jnp.where(prior, -_softplus(z), 0.0)
                hi, lo = _split_hi_lo(lk)
                later = c_prev + jnp.dot(jnp.concatenate([hi, lo], axis=1), t2, preferred_element_type=F32)
                p_sc[:, pl.ds(c0, page)] = jnp.where(prior, jnp.exp(z + lk + later), 0.0)
                return c_prev + jnp.sum(lk, axis=-1, keepdims=True)

            lax.fori_loop(0, n_chunks, body, jnp.zeros((rows, 1), F32))
        else:
            dist = (qpos - col).astype(F32)
            ok = dist >= 0.0
            if kind == KIND_MOBA:
                own = past // MOBA_BLOCK
                lane = lax.broadcasted_iota(jnp.int32, gate_sc.shape, 1)
                sel = _top_select(gate_sc[...], lane < own, axis=1)
                sel = jnp.where(lane == own, 1.0, sel)
                blk_row = lax.broadcasted_iota(jnp.int32, (gate_sc.shape[1], 1), 0)
                expand = jnp.where(col // MOBA_BLOCK == blk_row, 1.0, 0.0).astype(BF16)
                picked = jnp.dot(sel.astype(BF16), expand, preferred_element_type=F32)
                ok = jnp.logical_and(ok, picked > 0.5)
            sb = jnp.where(ok, s_sc[...] - slope_ref[...] * dist, NEG_INF)
            e = jnp.exp(sb - jnp.max(sb, axis=-1, keepdims=True))
            p = e / jnp.sum(e, axis=-1, keepdims=True)
            if kind == KIND_DIFF:
                lam = _diff_lambda(lq1, lk1, lq2, lk2, lam_init)
                half = REPL // 2
                for qi in range(n_new):
                    p_sc[qi * half:(qi + 1) * half, :] = (
                        p[qi * REPL:qi * REPL + half] - lam * p[qi * REPL + half:(qi + 1) * REPL])
            else:
                p_sc[...] = p
        acc_sc[0:out_rows, :] = jnp.dot(p_sc[0:out_rows, past:past + page].astype(BF16), vnew_ref[...],
                                        preferred_element_type=F32)

    @pl.when(step >= nkc)
    def _():
        for pp in range(0, pages, 2):
            vv = jnp.concatenate([v_refs[pp][...].astype(BF16), v_refs[pp + 1][...].astype(BF16)], axis=0)
            col0 = pl.multiple_of(((step - nkc) * pages + pp) * page, 2 * page)
            w = p_sc[0:out_rows, pl.ds(col0, 2 * page)].astype(BF16)
            acc_sc[0:out_rows, :] = acc_sc[0:out_rows, :] + jnp.dot(w, vv, preferred_element_type=F32)

    @pl.when(step == 2 * nkc - 1)
    def _():
        per_q = out_rows // n_new
        seg_w = LANES if kind == KIND_DIFF else HEAD_DIM
        r = lax.broadcasted_iota(jnp.int32, (per_q, 1), 0)
        col = lax.broadcasted_iota(jnp.int32, (1, d), 1)
        own_cols = col // seg_w == r
        for qi in range(n_new):
            a = jnp.where(own_cols, acc_sc[qi * per_q:(qi + 1) * per_q, :], 0.0)
            if kind == KIND_DIFF:
                ms = jnp.sum(a * a, axis=-1, keepdims=True) * (1.0 / LANES)
                a = a * lax.rsqrt(ms + LN_EPS) * gain_ref[...] * (1.0 - lam_init)
            o_ref[qi:qi + 1, :] = jnp.sum(a, axis=0, keepdims=True)


def _sample_stream_kernel(pt_ref, qrep_ref, knew_ref, vnew_ref, slope_ref, lq1, lk1, lq2, lk2, gain_ref,
                          *rest, pages, n_pages, past, n_new, lam_init):
    k_refs = rest[:pages]
    v_refs = rest[pages:2 * pages]
    o_ref = rest[2 * pages]
    wq_sc, m_sc, l_sc, acc_sc = rest[2 * pages + 1:]
    del pt_ref
    step = pl.program_id(1)
    page = k_refs[0].shape[0]
    rows, d = acc_sc.shape
    row_id = lax.broadcasted_iota(jnp.int32, (rows, 1), 0)
    qpos = past + row_id // REPL

    def fold(kk, vv, col0, newest):
        n = kk.shape[0]
        s = _dot_nt(wq_sc[...], kk)
        col = col0 + lax.broadcasted_iota(jnp.int32, (1, n), 1)
        dist = (qpos - col).astype(F32)
        sb = s - slope_ref[...] * dist
        if newest:
            sb = jnp.where(dist >= 0.0, sb, NEG_INF)
        m_prev = m_sc[...]
        m_new = jnp.maximum(m_prev, jnp.max(sb, axis=-1, keepdims=True))
        p = jnp.exp(sb - m_new)
        alpha = jnp.exp(m_prev - m_new)
        l_sc[...] = alpha * l_sc[...] + jnp.sum(p, axis=-1, keepdims=True)
        acc_sc[...] = alpha * acc_sc[...] + jnp.dot(p.astype(BF16), vv, preferred_element_type=F32)
        m_sc[...] = m_new

    @pl.when(step == 0)
    def _():
        col = lax.broadcasted_iota(jnp.int32, (1, d), 1)
        wq = jnp.where(col // HEAD_DIM == _row_segment(row_id, KIND_DIFF), qrep_ref[...], 0.0)
        wq_sc[...] = (wq * QK_SCALE).astype(BF16)
        m_sc[...] = jnp.full(m_sc.shape, NEG_INF, F32)
        l_sc[...] = jnp.zeros(l_sc.shape, F32)
        acc_sc[...] = jnp.zeros(acc_sc.shape, F32)
        fold(knew_ref[...], vnew_ref[...], past, True)

    kk = jnp.concatenate([k_refs[p][...].astype(BF16) for p in reversed(range(pages))], axis=0)
    vv = jnp.concatenate([v_refs[p][...].astype(BF16) for p in reversed(range(pages))], axis=0)
    fold(kk, vv, (n_pages - (step + 1) * pages) * page, False)

    @pl.when(step == n_pages // pages - 1)
    def _():
        col = lax.broadcasted_iota(jnp.int32, (1, d), 1)
        lam = _diff_lambda(lq1, lk1, lq2, lk2, lam_init)
        half = REPL // 2
        r = lax.broadcasted_iota(jnp.int32, (half, 1), 0)
        own_cols = col // LANES == r
        for qi in range(n_new):
            lo, mid, hi = qi * REPL, qi * REPL + half, (qi + 1) * REPL
            w = acc_sc[lo:mid, :] / l_sc[lo:mid, :] - lam * (acc_sc[mid:hi, :] / l_sc[mid:hi, :])
            a = jnp.where(own_cols, w, 0.0)
            ms = jnp.sum(a * a, axis=-1, keepdims=True) * (1.0 / LANES)
            a = a * lax.rsqrt(ms + LN_EPS) * gain_ref[...] * (1.0 - lam_init)
            o_ref[qi:qi + 1, :] = jnp.sum(a, axis=0, keepdims=True)


def _sb_fold(wq, kk, vv, col0, qpos, newest, c_ref, acc_ref):
    n = kk.shape[0]
    t2 = _suffix_matrix(LANES)
    s = _dot_nt(wq, kk)
    col = col0 + lax.broadcasted_iota(jnp.int32, (1, n), 1)
    c = c_ref[...]
    parts = []
    for hh in reversed(range(n // LANES)):
        z = s[:, hh * LANES:(hh + 1) * LANES]
        lk = -_softplus(z)
        if newest:
            prior = col[:, hh * LANES:(hh + 1) * LANES] < qpos
            lk = jnp.where(prior, lk, 0.0)
        later = c + jnp.dot(jnp.concatenate(_split_hi_lo(lk), axis=1), t2, preferred_element_type=F32)
        a = jnp.exp(z + lk + later)
        parts.append(jnp.where(prior, a, 0.0) if newest else a)
        c = c + jnp.sum(lk, axis=-1, keepdims=True)
    a_all = parts[0] if len(parts) == 1 else jnp.concatenate(parts[::-1], axis=1)
    acc_ref[...] = acc_ref[...] + jnp.dot(a_all.astype(BF16), vv, preferred_element_type=F32)
    c_ref[...] = c


SB_CHUNK_PAGES = 2


def _sb_sample_kernel(pt_ref, qrep_ref, knew_ref, vnew_ref, ck_ref, cv_ref, o_ref,
                      wq_sc, c_sc, acc_sc, kbuf, vbuf, sem, *, layer, n_pages, past, n_new):
    b = pl.program_id(0)
    rows, d = acc_sc.shape
    page = kbuf.shape[1] // SB_CHUNK_PAGES
    n_chunks = n_pages // SB_CHUNK_PAGES
    row_id = lax.broadcasted_iota(jnp.int32, (rows, 1), 0)
    qpos = past + row_id // REPL

    def copies(t, slot):
        out = []
        for u in range(SB_CHUNK_PAGES):
            pg = pt_ref[b * n_pages + SB_CHUNK_PAGES * t + u]
            dst = pl.ds(u * page, page)
            out.append(pltpu.make_async_copy(ck_ref.at[layer, pg], kbuf.at[slot, dst], sem.at[slot, u]))
            out.append(pltpu.make_async_copy(cv_ref.at[layer, pg], vbuf.at[slot, dst],
                                             sem.at[slot, SB_CHUNK_PAGES + u]))
        return out

    for cp in copies(n_chunks - 1, 0):
        cp.start()
    col = lax.broadcasted_iota(jnp.int32, (1, d), 1)
    wq = jnp.where(col // HEAD_DIM == _row_segment(row_id, KIND_SB), qrep_ref[...], 0.0)
    wq_sc[...] = (wq * QK_SCALE).astype(BF16)
    c_sc[...] = jnp.zeros(c_sc.shape, F32)
    acc_sc[...] = jnp.zeros(acc_sc.shape, F32)
    _sb_fold(wq_sc[...], knew_ref[...], vnew_ref[...], past, qpos, True, c_sc, acc_sc)

    def is_live():
        return (jnp.max(c_sc[...]) > SB_LOG_ZERO).astype(jnp.int32)

    def cond(carry):
        t, live = carry
        return jnp.logical_and(t >= 0, live > 0)

    def body(carry):
        t, _ = carry
        slot = (n_chunks - 1 - t) & 1
        for cp in copies(t, slot):
            cp.wait()

        @pl.when(t >= 1)
        def _():
            for cp in copies(t - 1, slot ^ 1):
                cp.start()

        _sb_fold(wq_sc[...], kbuf[slot].astype(BF16), vbuf[slot].astype(BF16),
                 t * SB_CHUNK_PAGES * page, qpos, False, c_sc, acc_sc)
        return t - 1, is_live()

    t_end, _ = lax.while_loop(cond, body, (n_chunks - 1, is_live()))

    @pl.when(t_end >= 0)
    def _():
        for cp in copies(t_end, (n_chunks - 1 - t_end) & 1):
            cp.wait()

    r = lax.broadcasted_iota(jnp.int32, (REPL, 1), 0)
    own_cols = col // HEAD_DIM == r
    for qi in range(n_new):
        a = jnp.where(own_cols, acc_sc[qi * REPL:(qi + 1) * REPL, :], 0.0)
        o_ref[qi:qi + 1, :] = jnp.sum(a, axis=0, keepdims=True)


def _sb_sample_mixer(layer, page_table, q, k_new, v_new, cache_k, cache_v):
    db, n_new, d = q.shape
    n_pages = page_table.shape[1]
    page = cache_k.shape[2]
    assert page == LANES and n_pages % SB_CHUNK_PAGES == 0 and n_new <= page and (n_new * REPL) % 16 == 0
    qrep = jnp.repeat(q, REPL, axis=1)
    pad = ((0, 0), (0, page - n_new), (0, 0))
    return _sb_sample_call(layer, db, n_new, d, n_pages, page)(
        page_table.reshape(-1), qrep, jnp.pad(k_new, pad).astype(BF16), jnp.pad(v_new, pad).astype(BF16),
        cache_k, cache_v)


def _sb_sample_call(layer, db, n_new, d, n_pages, page):
    rows = n_new * REPL

    def per_sample(shape):
        return pl.BlockSpec((None,) + shape, lambda b, pt: (b, 0, 0))

    hbm = pl.BlockSpec(memory_space=pl.ANY)
    chunk = SB_CHUNK_PAGES * page
    grid_spec = pltpu.PrefetchScalarGridSpec(
        num_scalar_prefetch=1,
        grid=(db,),
        in_specs=[per_sample((rows, d)), per_sample((page, d)), per_sample((page, d)), hbm, hbm],
        out_specs=per_sample((n_new, d)),
        scratch_shapes=[pltpu.VMEM((rows, d), BF16), pltpu.VMEM((rows, 1), F32), pltpu.VMEM((rows, d), F32),
                        pltpu.VMEM((2, chunk, d), F32), pltpu.VMEM((2, chunk, d), F32),
                        pltpu.SemaphoreType.DMA((2, 2 * SB_CHUNK_PAGES))],
    )
    return pl.pallas_call(
        functools.partial(_sb_sample_kernel, layer=layer, n_pages=n_pages, past=n_pages * page, n_new=n_new),
        grid_spec=grid_spec,
        out_shape=jax.ShapeDtypeStruct((db, n_new, d), F32),
        compiler_params=pltpu.CompilerParams(vmem_limit_bytes=VMEM_LIMIT, dimension_semantics=("arbitrary",)),
        name="sb_sample",
    )


def _pages_per_step(n_pages):
    return next(p for p in (16, 8, 2) if n_pages % p == 0)


def _sample_stream_mixer(layer, page_table, q, k_new, v_new, cache_k, cache_v, lam_vecs, sub_gain):
    db, n_new, d = q.shape
    n_pages = page_table.shape[1]
    page = cache_k.shape[2]
    past = n_pages * page
    pages = _pages_per_step(n_pages)
    rows = n_new * REPL
    assert page == LANES and n_pages % pages == 0 and n_new <= page and rows % 16 == 0
    lam_init = 0.8 - 0.6 * math.exp(-0.3 * layer)

    qrep = jnp.repeat(q, REPL, axis=1)
    pad = ((0, 0), (0, page - n_new), (0, 0))
    knew = jnp.pad(k_new, pad).astype(BF16)
    vnew = jnp.pad(v_new, pad).astype(BF16)
    slopes = _alibi_slopes(DIFF_HEADS)[(np.arange(rows) % REPL) % DIFF_HEADS]
    slope_rows = jnp.asarray(slopes.reshape(rows, 1))
    gain_full = jnp.tile(sub_gain, (1, d // LANES))

    def per_sample(shape):
        return pl.BlockSpec((None,) + shape, lambda b, s, pt: (b, 0, 0))

    def const(shape):
        return pl.BlockSpec(shape, lambda b, s, pt: (0, 0))

    def page_spec(p):
        return pl.BlockSpec((None, None, page, d),
                            lambda b, s, pt: (layer, pt[b * n_pages + n_pages - 1 - s * pages - p], 0, 0))

    vec = const((1, HEAD_DIM))
    grid_spec = pltpu.PrefetchScalarGridSpec(
        num_scalar_prefetch=1,
        grid=(db, n_pages // pages),
        in_specs=[per_sample((rows, d)), per_sample((page, d)), per_sample((page, d)),
                  const((rows, 1)), vec, vec, vec, vec, const((1, d))]
                 + [page_spec(p) for p in range(pages)] * 2,
        out_specs=per_sample((n_new, d)),
        scratch_shapes=[pltpu.VMEM((rows, d), BF16), pltpu.VMEM((rows, 1), F32), pltpu.VMEM((rows, 1), F32),
                        pltpu.VMEM((rows, d), F32)],
    )
    return pl.pallas_call(
        functools.partial(_sample_stream_kernel, pages=pages, n_pages=n_pages, past=past,
                          n_new=n_new, lam_init=lam_init),
        grid_spec=grid_spec,
        out_shape=jax.ShapeDtypeStruct((db, n_new, d), F32),
        compiler_params=pltpu.CompilerParams(vmem_limit_bytes=VMEM_LIMIT,
                                             dimension_semantics=("arbitrary", "arbitrary")),
        name="diff_sample",
    )(page_table.reshape(-1), qrep, knew, vnew, slope_rows, *lam_vecs, gain_full,
      *([cache_k] * pages), *([cache_v] * pages))


def _sample_mixer(kind, layer, page_table, q, k_new, v_new, cache_k, cache_v, lam_vecs, sub_gain):
    if kind == KIND_SB:
        return _sb_sample_mixer(layer, page_table, q, k_new, v_new, cache_k, cache_v)
    if kind == KIND_DIFF:
        return _sample_stream_mixer(layer, page_table, q, k_new, v_new, cache_k, cache_v, lam_vecs, sub_gain)
    db, n_new, d = q.shape
    n_pages = page_table.shape[1]
    page = cache_k.shape[2]
    past = n_pages * page
    pages = _pages_per_step(n_pages)
    nkc = n_pages // pages
    rows = n_new * REPL
    width = past + page
    assert MOBA_BLOCK == 2 * page and past % MOBA_BLOCK == 0 and past // MOBA_BLOCK < LANES
    assert n_pages % pages == 0 and n_new <= page and rows % 16 == 0
    lam_init = 0.8 - 0.6 * math.exp(-0.3 * layer)

    qrep = jnp.repeat(q, REPL, axis=1)
    pad = ((0, 0), (0, page - n_new), (0, 0))
    knew = jnp.pad(k_new, pad).astype(BF16)
    vnew = jnp.pad(v_new, pad).astype(BF16)
    rep = np.arange(rows) % REPL
    if kind == KIND_DIFF:
        slopes = _alibi_slopes(DIFF_HEADS)[rep % DIFF_HEADS]
    elif kind == KIND_MOBA:
        slopes = _alibi_slopes(MOBA_HEADS)[rep]
    else:
        slopes = np.zeros((rows,), np.float32)
    slope_rows = jnp.asarray(slopes.reshape(rows, 1))
    gain_full = jnp.tile(sub_gain, (1, d // LANES))

    def per_sample(shape):
        return pl.BlockSpec((None,) + shape, lambda b, s, pt: (b, 0, 0))

    def const(shape):
        return pl.BlockSpec(shape, lambda b, s, pt: (0, 0))

    def k_spec(p):
        return pl.BlockSpec(
            (None, None, page, d),
            lambda b, s, pt: (layer, pt[b * n_pages + jnp.minimum(s, nkc - 1) * pages + p], 0, 0))

    def v_spec(p):
        return pl.BlockSpec(
            (None, None, page, d),
            lambda b, s, pt: (layer, pt[b * n_pages + jnp.maximum(s - nkc, 0) * pages + p], 0, 0))

    vec = const((1, HEAD_DIM))
    grid_spec = pltpu.PrefetchScalarGridSpec(
        num_scalar_prefetch=1,
        grid=(db, 2 * nkc),
        in_specs=[per_sample((rows, d)), per_sample((page, d)), per_sample((page, d)),
                  const((rows, 1)), vec, vec, vec, vec, const((1, d))]
                 + [k_spec(p) for p in range(pages)] + [v_spec(p) for p in range(pages)],
        out_specs=per_sample((n_new, d)),
        scratch_shapes=[pltpu.VMEM((rows, d), BF16), pltpu.VMEM((rows, d), F32),
                        pltpu.VMEM((rows, width), F32), pltpu.VMEM((rows, width), F32),
                        pltpu.VMEM((rows, d), F32), pltpu.VMEM((rows, LANES), F32)],
    )
    return pl.pallas_call(
        functools.partial(_sample_kernel, kind=kind, pages=pages, nkc=nkc, past=past,
                          n_new=n_new, lam_init=lam_init),
        grid_spec=grid_spec,
        out_shape=jax.ShapeDtypeStruct((db, n_new, d), F32),
        compiler_params=pltpu.CompilerParams(vmem_limit_bytes=VMEM_LIMIT),
        name=("diff_sample", "sb_sample", "moba_sample")[kind],
    )(page_table.reshape(-1), qrep, knew, vnew, slope_rows, *lam_vecs, gain_full,
      *([cache_k] * pages), *([cache_v] * pages))


def kernel(x_prompt, x_sample, cache_k, cache_v, page_table, w_in, w_out, ln_gain, ln_bias,
           diff_lambda_q1, diff_lambda_k1, diff_lambda_q2, diff_lambda_k2, diff_subln_gain):
    depth = w_in.shape[0]
    b, s, d = x_prompt.shape
    db, ds, _ = x_sample.shape
    alpha = (2 * depth) ** 0.25
    tm_in = min(512, b * s)
    tm_out = min(512, b * s)
    xp = x_prompt.reshape(b * s, d)
    xs = x_sample.reshape(db * ds, d)
    k_all = jnp.zeros((depth, b * s, d), F32)
    v_all = jnp.zeros((depth, b * s, d), F32)
    ks_rows, vs_rows = [], []
    for i in range(depth):
        kind = i % N_MIXERS
        j = i // N_MIXERS
        lam_vecs = [v[j].reshape(1, HEAD_DIM) for v in
                    (diff_lambda_q1, diff_lambda_k1, diff_lambda_q2, diff_lambda_k2)]
        sub_gain = diff_subln_gain[j].reshape(1, LANES)
        w_in_b = w_in[i].astype(BF16)
        w_out_b = w_out[i].astype(BF16)

        qt, k_all, kb, v_all, vt, g = _in_proj_prompt(xp, w_in_b[:, 0:d].T, w_in_b[:, 2 * d:3 * d].T,
                                                      w_in_b[:, d:4 * d], tm_in, i, depth, k_all, v_all)
        o = _prompt_mixer(kind, i, b, s, qt, k_all.reshape(depth, b, s, d), kb.reshape(b, s, d), vt,
                          lam_vecs, sub_gain)
        xp = _out_proj(o.reshape(b * s, d), g, xp, w_out_b, ln_gain[i], ln_bias[i], alpha, tm_out)

        qs, ks, vs, gs = _in_proj_rows(xs, w_in_b, db * ds)
        sshp = (db, ds, d)
        os_ = _sample_mixer(kind, i, page_table, qs.reshape(sshp), ks.reshape(sshp), vs.reshape(sshp),
                            cache_k, cache_v, lam_vecs, sub_gain)
        xs = _out_proj(os_.reshape(db * ds, d), gs, xs, w_out_b, ln_gain[i], ln_bias[i], alpha, db * ds)
        ks_rows.append(ks.reshape(sshp))
        vs_rows.append(vs.reshape(sshp))
    return (xp.reshape(b, s, d), xs.reshape(db, ds, d), k_all.reshape(depth, b, s, d),
            v_all.reshape(depth, b, s, d), jnp.stack(ks_rows), jnp.stack(vs_rows))
```
